```python
import math
import jax
import jax.numpy as jnp
from jax import lax
import numpy as np

D_MODEL = 2048
BATCH = 16
SEQ = 2048
DEPTH = 4

GRID_W = 64
CTX_LEN = 256
EPS = 1e-6
F32 = jnp.float32

A_HEADS = 8
A_DIM = 64
A_VDIM = 2 * A_DIM
A_QK = A_HEADS * 2 * A_DIM
A_WIDTH = A_HEADS * A_VDIM
Q_BLOCK = 128
ROPE_BASE = 10000.0

DN_HEADS = 8
DN_DK = 128
DN_DV = 128
DN_QK = DN_HEADS * DN_DK
DN_WIDTH = DN_HEADS * DN_DV
DN_CONV = 5
DN_CHUNK = 64

HY_WIDTH = 1024
HY_CONV = 3
HY_BANDS = 16
HY_EMB = 1 + 2 * HY_BANDS
HY_HIDDEN = 64
HY_DECAY_MIN = math.log(1e-2) / 1.5
HY_DECAY_MAX = math.log(1e-2) / 0.3

N_BRANCH = 3
N_EXPERTS = 16
EXPERT_FF = 1024
EC_CAPACITY = 2

PROJ_SIZES = (A_QK, A_QK, A_WIDTH, DN_QK, DN_QK, DN_WIDTH, DN_WIDTH, 2 * DN_HEADS, 2 * DN_HEADS, 3 * HY_WIDTH, N_BRANCH * D_MODEL)
PROJ_TOTAL = sum(PROJ_SIZES)

kernel_name = 'hybrid_diffattn_gdn_hyena_ec_block'


def rmsnorm(x, g):
    xf = x.astype(F32)
    y = xf * lax.rsqrt(jnp.mean(xf * xf, axis=-1, keepdims=True) + EPS)
    return y.astype(x.dtype) * g


def l2norm(t):
    tf = t.astype(F32)
    return (tf * lax.rsqrt(jnp.sum(tf * tf, axis=-1, keepdims=True) + EPS)).astype(t.dtype)


def modulate(h, shift, scale):
    return h * (1.0 + scale) + shift


def split_proj(p):
    offs = [int(o) for o in np.cumsum(PROJ_SIZES)[:-1]]
    return jnp.split(p, offs, axis=-1)


def short_conv(x, w):
    taps = w.shape[1]
    pad = (taps - 1) // 2
    return lax.conv_general_dilated(
        x, jnp.transpose(w)[:, None, :].astype(x.dtype), window_strides=(1,), padding=[(pad, pad)],
        dimension_numbers=('NWC', 'WIO', 'NWC'), feature_group_count=x.shape[-1])


def axial_rope(n_tok):
    t = jnp.arange(n_tok, dtype=jnp.int32)
    row = (t // GRID_W).astype(F32)
    col = (t % GRID_W).astype(F32)
    n_freq = A_DIM // 4
    inv = ROPE_BASE ** (-jnp.arange(n_freq, dtype=F32) / n_freq)
    ang = jnp.concatenate([row[:, None] * inv, col[:, None] * inv], axis=-1)
    return jnp.cos(ang), jnp.sin(ang)


def apply_rope(x, cos, sin):
    c = cos[None, :, None, None, :]
    s = sin[None, :, None, None, :]
    x1, x2 = x[..., :A_DIM // 2], x[..., A_DIM // 2:]
    return jnp.concatenate([x1 * c - x2 * s, x1 * s + x2 * c], axis=-1).astype(x.dtype)


def diff_attend(q, k, v, lam):
    s = jnp.einsum('bqhmd,bshmd->bhmqs', q, k).astype(F32) * (A_DIM ** -0.5)
    p = jax.nn.softmax(s, axis=-1)
    a = p[:, :, 0] - lam * p[:, :, 1]
    return jnp.einsum('bhqs,bshe->bqhe', a.astype(v.dtype), v)


def diff_attention_branch(qc, kc, vc, ql, kl, vl, lam_p, lam_init, g_norm, with_ctx):
    bsz, n_lat = ql.shape[:2]
    qk_heads = lambda t: t.reshape(t.shape[0], t.shape[1], A_HEADS, 2, A_DIM)
    v_heads = lambda t: t.reshape(t.shape[0], t.shape[1], A_HEADS, A_VDIM)
    cos, sin = axial_rope(n_lat)
    ql = apply_rope(qk_heads(ql), cos, sin)
    kl = apply_rope(qk_heads(kl), cos, sin)
    kc, vc, vl = qk_heads(kc), v_heads(vc), v_heads(vl)
    lp = lam_p.astype(F32)
    lam = jnp.exp(jnp.sum(lp[0] * lp[1])) - jnp.exp(jnp.sum(lp[2] * lp[3])) + lam_init
    k_all = jnp.concatenate([kc, kl], axis=1)
    v_all = jnp.concatenate([vc, vl], axis=1)
    n_blk = n_lat // Q_BLOCK
    q_blocks = jnp.moveaxis(ql.reshape(bsz, n_blk, Q_BLOCK, A_HEADS, 2, A_DIM), 1, 0)
    o_l = lax.map(lambda qb: diff_attend(qb, k_all, v_all, lam), q_blocks)
    o_l = jnp.moveaxis(o_l, 0, 1).reshape(bsz, n_lat, A_HEADS, A_VDIM)
    post = lambda o: (rmsnorm(o, g_norm) * (1.0 - lam_init)).reshape(o.shape[0], o.shape[1], A_WIDTH)
    y_l = post(o_l)
    y_c = post(diff_attend(qk_heads(qc), kc, vc, lam)) if with_ctx else None
    return y_c, y_l


def gdn_chunked(q, k, v, g, beta, s0):
    bsz, n, nh, _ = q.shape
    dv = v.shape[-1]
    cs = DN_CHUNK
    nc = n // cs
    chunk = lambda t: jnp.moveaxis(t.reshape(bsz, nc, cs, nh, *t.shape[3:]), 2, 3)
    qc, kc, vc = chunk(q), chunk(k), chunk(v)
    gc, bc = chunk(g.astype(F32)), chunk(beta.astype(F32))
    gcum = jnp.cumsum(gc, axis=-1)
    lower = jnp.tril(jnp.ones((cs, cs), bool))
    strict = jnp.tril(jnp.ones((cs, cs), bool), -1)
    seg = gcum[..., :, None] - gcum[..., None, :]
    decay = jnp.where(lower, jnp.exp(jnp.where(lower, seg, 0.0)), 0.0)
    kk = jnp.einsum('bnhcd,bnhsd->bnhcs', kc, kc)
    a_mat = jnp.where(strict, bc[..., :, None] * kk * decay, 0.0)
    eye = jnp.eye(cs, dtype=a_mat.dtype)
    t_mat = lax.linalg.triangular_solve(eye + a_mat, jnp.broadcast_to(eye, a_mat.shape),
                                        left_side=True, lower=True, unit_diagonal=True)
    u = jnp.einsum('bnhcs,bnhse->bnhce', t_mat, vc * bc[..., None])
    w = jnp.einsum('bnhcs,bnhsd->bnhcd', t_mat, kc * (bc * jnp.exp(gcum))[..., None])
    qk = jnp.einsum('bnhcd,bnhsd->bnhcs', qc, kc) * decay
    q_dec = qc * jnp.exp(gcum)[..., None]
    k_dec = kc * jnp.exp(gcum[..., -1:] - gcum)[..., None]
    g_last = jnp.exp(gcum[..., -1])

    def step(state, xs):
        u_i, w_i, qk_i, q_i, k_i, gl_i = xs
        v_new = u_i - jnp.einsum('bhcd,bhde->bhce', w_i, state)
        o_i = jnp.einsum('bhcd,bhde->bhce', q_i, state) + jnp.einsum('bhcs,bhse->bhce', qk_i, v_new)
        state = state * gl_i[..., None, None] + jnp.einsum('bhcd,bhce->bhde', k_i, v_new)
        return state, o_i

    xs = tuple(jnp.moveaxis(t, 1, 0) for t in (u, w, qk, q_dec, k_dec, g_last))
    s_fin, o = lax.scan(step, s0, xs)
    o = jnp.transpose(o, (1, 0, 3, 2, 4)).reshape(bsz, n, nh, dv)
    return o, s_fin


def deltanet_branch(parts_c, parts_l, conv_w, a_log, dt_bias, g_norm, with_ctx):
    def prep(q, k, v, a, b):
        bsz, n = q.shape[:2]
        qkv = jax.nn.silu(short_conv(jnp.concatenate([q, k, v], axis=-1), conv_w))
        q, k, v = jnp.split(qkv, [DN_QK, 2 * DN_QK], axis=-1)
        q = (l2norm(q.reshape(bsz, n, DN_HEADS, DN_DK)) * (DN_DK ** -0.5)).astype(q.dtype)
        k = l2norm(k.reshape(bsz, n, DN_HEADS, DN_DK))
        v = v.reshape(bsz, n, DN_HEADS, DN_DV)
        a = a.reshape(bsz, n, 2, DN_HEADS).astype(F32)
        b = b.reshape(bsz, n, 2, DN_HEADS).astype(F32)
        g = -jnp.exp(a_log.astype(F32)) * jax.nn.softplus(a + dt_bias.astype(F32))
        return q, k, v, g, jax.nn.sigmoid(b)

    qc, kc, vc, gc, bc = prep(parts_c[0], parts_c[1], parts_c[2], parts_c[4], parts_c[5])
    ql, kl, vl, gl, bl = prep(parts_l[0], parts_l[1], parts_l[2], parts_l[4], parts_l[5])
    rev = lambda t: t[:, ::-1]
    s0 = jnp.zeros((qc.shape[0], DN_HEADS, DN_DK, DN_DV), F32)
    oc_f, sc_f = gdn_chunked(qc, kc, vc, gc[:, :, 0], bc[:, :, 0], s0)
    ol_f, _ = gdn_chunked(ql, kl, vl, gl[:, :, 0], bl[:, :, 0], sc_f)
    oc_b, sc_b = gdn_chunked(rev(qc), rev(kc), rev(vc), rev(gc[:, :, 1]), rev(bc[:, :, 1]), s0)
    ol_b, _ = gdn_chunked(rev(ql), rev(kl), rev(vl), rev(gl[:, :, 1]), rev(bl[:, :, 1]), sc_b)

    def post(o_f, o_b, z):
        bsz, n = z.shape[:2]
        o = (o_f + rev(o_b)).astype(z.dtype)
        zz = z.reshape(bsz, n, DN_HEADS, DN_DV)
        return (rmsnorm(o, g_norm) * jax.nn.silu(zz)).reshape(bsz, n, DN_WIDTH)

    y_l = post(ol_f, ol_b, parts_l[3])
    y_c = post(oc_f, oc_b, parts_c[3]) if with_ctx else None
    return y_c, y_l


def hyena_filters(n_tok, w1, b1, w2, b2, freq, w3):
    t = jnp.linspace(0.0, 1.0, n_tok, dtype=F32)[:, None]
    pos = jnp.arange(n_tok, dtype=F32)
    bands = jnp.linspace(1e-4, HY_BANDS - 1, HY_BANDS, dtype=F32)
    ang = (2.0 * math.pi / n_tok) * pos[:, None] * bands[None, :]
    feat = jnp.concatenate([t, jnp.cos(ang), -jnp.sin(ang)], axis=-1)
    h = jnp.sin(freq[0] * (feat @ w1 + b1))
    h = jnp.sin(freq[1] * (h @ w2 + b2))
    h = (h @ w3).astype(F32).reshape(n_tok, 2, HY_WIDTH)
    rates = jnp.abs(jnp.linspace(HY_DECAY_MIN, HY_DECAY_MAX, HY_WIDTH, dtype=F32))
    h = h * jnp.exp(-t * rates)[:, None, :]
    circ = jnp.concatenate([h[:, 0], jnp.zeros((1, HY_WIDTH), F32), h[:0:-1, 1]], axis=0)
    return circ / jnp.sum(jnp.abs(circ), axis=0, keepdims=True)


def bidir_long_conv(u, circ):
    n = u.shape[1]
    uf = jnp.fft.rfft(u.astype(F32), n=2 * n, axis=1)
    kf = jnp.fft.rfft(circ, axis=0)
    y = jnp.fft.irfft(uf * kf[None], n=2 * n, axis=1)[:, :n]
    return y.astype(u.dtype)


def hyena_branch(proj, conv_w, circ, d_skip):
    uc = short_conv(proj, conv_w)
    v, x1, x0 = jnp.split(uc, 3, axis=-1)
    u = x1 * v
    z = bidir_long_conv(u, circ) + d_skip * u
    return x0 * z


def merge_branches(branches, gate_cols, w_branch, w_out):
    bsz, n = gate_cols.shape[:2]
    gates = jax.nn.sigmoid(gate_cols.reshape(bsz, n, N_BRANCH, D_MODEL))
    y = gates[:, :, 0] * (branches[0] @ w_branch[0])
    for r in range(1, N_BRANCH):
        y = y + gates[:, :, r] * (branches[r] @ w_branch[r])
    return y @ w_out


def expert_choice_ffn(h, router, w_gate, w_up, w_down):
    n_tok, d = h.shape[1], h.shape[2]
    cap = EC_CAPACITY * n_tok // N_EXPERTS
    aff = jax.nn.softmax(jnp.einsum('bnd,de->bne', h, router).astype(F32), axis=-1)
    gate, idx = lax.top_k(jnp.swapaxes(aff, 1, 2), cap)
    xe = jax.vmap(lambda hb, ib: hb[ib])(h, idx)
    a = jnp.einsum('becd,edf->becf', xe, w_gate)
    u = jnp.einsum('becd,edf->becf', xe, w_up)
    ye = jnp.einsum('becf,efd->becd', jax.nn.silu(a) * u, w_down) * gate[..., None].astype(h.dtype)
    scatter = lambda yb, ib: jnp.zeros((n_tok, d), h.dtype).at[ib.reshape(-1)].add(yb.reshape(-1, d).astype(h.dtype))
    return jax.vmap(scatter)(ye, idx)


def token_mixer(hc, hl, w_in, lam_p, lam_init, diff_g, dn_conv, dn_a_log, dn_dt_bias, dn_g,
                hy_conv, hy_w1, hy_b1, hy_w2, hy_b2, hy_freq, hy_w3, hy_bias, w_branch, w_out, with_ctx):
    aq_c, ak_c, av_c, dq_c, dk_c, dv_c, dz_c, da_c, db_c, hy_c, gt_c = split_proj(hc @ w_in)
    aq_l, ak_l, av_l, dq_l, dk_l, dv_l, dz_l, da_l, db_l, hy_l, gt_l = split_proj(hl @ w_in)
    ya_c, ya_l = diff_attention_branch(aq_c, ak_c, av_c, aq_l, ak_l, av_l, lam_p, lam_init, diff_g, with_ctx)
    yd_c, yd_l = deltanet_branch((dq_c, dk_c, dv_c, dz_c, da_c, db_c), (dq_l, dk_l, dv_l, dz_l, da_l, db_l),
                                 dn_conv, dn_a_log, dn_dt_bias, dn_g, with_ctx)
    circ_l = hyena_filters(hl.shape[1], hy_w1, hy_b1, hy_w2, hy_b2, hy_freq, hy_w3)
    yh_l = hyena_branch(hy_l, hy_conv, circ_l, hy_bias)
    y_l = merge_branches((ya_l, yd_l, yh_l), gt_l, w_branch, w_out)
    y_c = None
    if with_ctx:
        circ_c = hyena_filters(hc.shape[1], hy_w1, hy_b1, hy_w2, hy_b2, hy_freq, hy_w3)
        yh_c = hyena_branch(hy_c, hy_conv, circ_c, hy_bias)
        y_c = merge_branches((ya_c, yd_c, yh_c), gt_c, w_branch, w_out)
    return y_c, y_l


def setup_inputs(seed: int = 0) -> dict:
    key = jax.random.key(seed)
    ks = jax.random.split(key, 30)
    nrm = lambda k, shape, scale: jax.random.normal(k, shape, F32) * scale
    gain = lambda k, shape: 1.0 + 0.02 * jax.random.normal(k, shape, F32)
    dt = jnp.exp(jax.random.uniform(ks[13], (DEPTH, 2, DN_HEADS), F32, math.log(1e-3), math.log(1e-1)))
    return {
        'x': nrm(ks[0], (BATCH, SEQ, D_MODEL), 1.0),
        'c': nrm(ks[1], (BATCH, D_MODEL), 1.0),
        'ctx': nrm(ks[2], (BATCH, CTX_LEN, D_MODEL), 1.0),
        'c_ctx': nrm(ks[3], (D_MODEL,), 1.0),
        'w_ada': nrm(ks[4], (DEPTH, D_MODEL, 6 * D_MODEL), 0.5 * D_MODEL ** -0.5),
        'b_ada': nrm(ks[5], (DEPTH, 6 * D_MODEL), 0.02),
        'norm_mix': gain(ks[6], (DEPTH, D_MODEL)),
        'norm_ffn': gain(ks[7], (DEPTH, D_MODEL)),
        'w_in': nrm(ks[8], (DEPTH, D_MODEL, PROJ_TOTAL), D_MODEL ** -0.5),
        'diff_lambda': nrm(ks[9], (DEPTH, 4, A_DIM), 0.1),
        'diff_norm': gain(ks[10], (DEPTH, A_VDIM)),
        'dn_conv': nrm(ks[11], (DEPTH, 2 * DN_QK + DN_WIDTH, DN_CONV), DN_CONV ** -0.5),
        'dn_a_log': jnp.log(jax.random.uniform(ks[12], (DEPTH, 2, DN_HEADS), F32, 1.0, 16.0)),
        'dn_dt_bias': dt + jnp.log(-jnp.expm1(-dt)),
        'dn_norm': gain(ks[14], (DEPTH, DN_DV)),
        'hy_conv': nrm(ks[15], (DEPTH, 3 * HY_WIDTH, HY_CONV), HY_CONV ** -0.5),
        'hy_w1': nrm(ks[16], (DEPTH, HY_EMB, HY_HIDDEN), HY_EMB ** -0.5),
        'hy_b1': nrm(ks[17], (DEPTH, HY_HIDDEN), 0.02),
        'hy_w2': nrm(ks[18], (DEPTH, HY_HIDDEN, HY_HIDDEN), HY_HIDDEN ** -0.5),
        'hy_b2': nrm(ks[19], (DEPTH, HY_HIDDEN), 0.02),
        'hy_freq': gain(ks[20], (DEPTH, 2, HY_HIDDEN)),
        'hy_w3': nrm(ks[21], (DEPTH, HY_HIDDEN, 2 * HY_WIDTH), HY_HIDDEN ** -0.5),
        'hy_bias': nrm(ks[22], (DEPTH, HY_WIDTH), 0.5),
        'w_branch': nrm(ks[23], (DEPTH, N_BRANCH, A_WIDTH, D_MODEL), A_WIDTH ** -0.5),
        'w_out': nrm(ks[24], (DEPTH, D_MODEL, D_MODEL), D_MODEL ** -0.5),
        'router': nrm(ks[25], (DEPTH, D_MODEL, N_EXPERTS), D_MODEL ** -0.5),
        'w_gate': nrm(ks[26], (DEPTH, N_EXPERTS, D_MODEL, EXPERT_FF), D_MODEL ** -0.5),
        'w_up': nrm(ks[27], (DEPTH, N_EXPERTS, D_MODEL, EXPERT_FF), D_MODEL ** -0.5),
        'w_down': nrm(ks[28], (DEPTH, N_EXPERTS, EXPERT_FF, D_MODEL), EXPERT_FF ** -0.5),
        'norm_final': gain(ks[29], (D_MODEL,)),
    }


def reference(x, c, ctx, c_ctx, w_ada, b_ada, norm_mix, norm_ffn, w_in, diff_lambda, diff_norm,
              dn_conv, dn_a_log, dn_dt_bias, dn_norm, hy_conv, hy_w1, hy_b1, hy_w2, hy_b2, hy_freq,
              hy_w3, hy_bias, w_branch, w_out, router, w_gate, w_up, w_down, norm_final):
    for l in range(DEPTH):
        with_ctx = l < DEPTH - 1
        lam_init = 0.8 - 0.6 * math.exp(-0.3 * l)
        mod_l = (jax.nn.silu(c) @ w_ada[l] + b_ada[l])[:, None, :]
        mod_c = (jax.nn.silu(c_ctx) @ w_ada[l] + b_ada[l])[None, None, :]
        sh1_l, sc1_l, g1_l, sh2_l, sc2_l, g2_l = jnp.split(mod_l, 6, axis=-1)
        sh1_c, sc1_c, g1_c, sh2_c, sc2_c, g2_c = jnp.split(mod_c, 6, axis=-1)
        hl = modulate(rmsnorm(x, norm_mix[l]), sh1_l, sc1_l)
        hc = modulate(rmsnorm(ctx, norm_mix[l]), sh1_c, sc1_c)
        y_c, y_l = token_mixer(hc, hl, w_in[l], diff_lambda[l], lam_init, diff_norm[l], dn_conv[l],
                               dn_a_log[l], dn_dt_bias[l], dn_norm[l], hy_conv[l], hy_w1[l], hy_b1[l],
                               hy_w2[l], hy_b2[l], hy_freq[l], hy_w3[l], hy_bias[l], w_branch[l], w_out[l],
                               with_ctx)
        x = x + g1_l * y_l
        hl = modulate(rmsnorm(x, norm_ffn[l]), sh2_l, sc2_l)
        x = x + g2_l * expert_choice_ffn(hl, router[l], w_gate[l], w_up[l], w_down[l])
        if with_ctx:
            ctx = ctx + g1_c * y_c
            hc = modulate(rmsnorm(ctx, norm_ffn[l]), sh2_c, sc2_c)
            ctx = ctx + g2_c * expert_choice_ffn(hc, router[l], w_gate[l], w_up[l], w_down[l])
    return rmsnorm(x, norm_final)
```

```python
import functools
import math

import jax
import jax.numpy as jnp
from jax import lax
from jax.experimental import pallas as pl
from jax.experimental.pallas import tpu as pltpu

F32 = jnp.float32
BF16 = jnp.bfloat16

D_MODEL = 2048
GRID_W = 64
EPS = 1e-6
A_HEADS = 8
A_DIM = 64
A_VDIM = 128
ROPE_BASE = 10000.0
DN_HEADS = 8
DN_DK = 128
DN_DV = 128
DN_CONV = 5
DN_CHUNK = 64
HY_WIDTH = 1024
HY_CONV = 3
HY_BANDS = 16
HY_EMB = 1 + 2 * HY_BANDS
HY_HIDDEN = 64
HY_DECAY_MIN = math.log(1e-2) / 1.5
HY_DECAY_MAX = math.log(1e-2) / 0.3
N_BRANCH = 3
N_EXPERTS = 16
EXPERT_FF = 1024
EC_CAPACITY = 2

LANES = 128
SUBLANES = 8
HALO = SUBLANES
VMEM_LIMIT = 56 * 1024 * 1024


def _params(sem, vmem=VMEM_LIMIT):
    return pltpu.CompilerParams(dimension_semantics=sem, vmem_limit_bytes=vmem)


def _tile(n, pref):
    if n <= pref:
        return n
    t = pref
    while n % t:
        t -= SUBLANES
    return t


def _split3(x):
    hi = x.astype(BF16)
    lo = (x - hi.astype(F32)).astype(BF16)
    return hi, lo


def _dot(a, b):
    return jnp.dot(a, b, preferred_element_type=F32)


def _dot3(a, b):
    ah, al = _split3(a)
    bh, bl = _split3(b)
    return _dot(ah, bh) + (_dot(ah, bl) + _dot(al, bh))


def _dot_nt(a, b):
    return lax.dot_general(a, b, (((1,), (1,)), ((), ())), preferred_element_type=F32)


def _dot3_nt(a, b):
    ah, al = _split3(a)
    bh, bl = _split3(b)
    return _dot_nt(ah, bh) + (_dot_nt(ah, bl) + _dot_nt(al, bh))


def _silu(x):
    return x * (1.0 / (1.0 + jnp.exp(-x)))


def _sigmoid(x):
    return 1.0 / (1.0 + jnp.exp(-x))


def _mm_kernel(*refs, nk, n_extra, epilogue, prologue, x3):
    a_ref, b_ref = refs[0], refs[1]
    extra_refs = refs[2:2 + n_extra]
    o_ref = refs[2 + n_extra]
    a = a_ref[0]
    b = b_ref[0]
    if prologue is not None:
        a = prologue(a)
    if x3:
        part = _dot3(a.astype(F32), b.astype(F32))
    else:
        part = _dot(a.astype(BF16), b.astype(BF16))

    def finish(acc):
        if epilogue is not None:
            acc = epilogue(acc, *[r[0] for r in extra_refs])
        o_ref[0] = acc.astype(o_ref.dtype)

    if nk == 1:
        finish(part)
    else:
        acc_ref = refs[3 + n_extra]
        k = pl.program_id(3)

        @pl.when(k == 0)
        def _():
            acc_ref[...] = part

        @pl.when(k > 0)
        def _():
            acc_ref[...] += part

        @pl.when(k == nk - 1)
        def _():
            finish(acc_ref[...])


def _mm(a, b, *, out_dtype=F32, tm=512, tn=512, tk=None, extras=(), epilogue=None, prologue=None,
        x3=False):
    ga, m, k = a.shape
    gb, kb, n = b.shape
    assert k == kb
    g = max(ga, gb)
    tm = _tile(m, tm)
    tn = n if n <= tn else tn
    tk = k if tk is None else min(tk, k)
    assert m % tm == 0 and n % tn == 0 and k % tk == 0, (m, n, k, tm, tn, tk)
    ni, nj, nk = m // tm, n // tn, k // tk
    in_specs = [
        pl.BlockSpec((1, tm, tk), lambda gi, j, i, kk: (gi if ga > 1 else 0, i, kk)),
        pl.BlockSpec((1, tk, tn), lambda gi, j, i, kk: (gi if gb > 1 else 0, kk, j)),
    ]
    args = [a, b]
    for arr, mode in extras:
        ge = arr.shape[0]
        if mode == "tile":
            spec = pl.BlockSpec((1, tm, tn), lambda gi, j, i, kk, ge=ge: (gi if ge > 1 else 0, i, j))
        elif mode == "row":
            spec = pl.BlockSpec((1, 1, tn), lambda gi, j, i, kk, ge=ge: (gi if ge > 1 else 0, 0, j))
        else:
            spec = pl.BlockSpec((1, tm, 1), lambda gi, j, i, kk, ge=ge: (gi if ge > 1 else 0, i, 0))
        in_specs.append(spec)
        args.append(arr)
    scratch = [pltpu.VMEM((tm, tn), F32)] if nk > 1 else []
    return pl.pallas_call(
        functools.partial(_mm_kernel, nk=nk, n_extra=len(extras), epilogue=epilogue,
                          prologue=prologue, x3=x3),
        grid=(g, nj, ni, nk),
        in_specs=in_specs,
        out_specs=pl.BlockSpec((1, tm, tn), lambda gi, j, i, kk: (gi, i, j)),
        out_shape=jax.ShapeDtypeStruct((g, m, n), out_dtype),
        scratch_shapes=scratch,
        compiler_params=_params(("parallel", "parallel", "parallel", "arbitrary")),
    )(*args)


def _normmod_body(x_ref, g_ref, mod_ref, shift_idx, scale_idx):
    x = x_ref[0]
    y = x * lax.rsqrt(jnp.mean(x * x, axis=-1, keepdims=True) + EPS) * g_ref[...]
    m = mod_ref[0]
    return y * (1.0 + m[scale_idx:scale_idx + 1]) + m[shift_idx:shift_idx + 1]


def _normmod_kernel(x_ref, g_ref, mod_ref, h_ref, *, shift_idx, scale_idx):
    h_ref[0] = _normmod_body(x_ref, g_ref, mod_ref, shift_idx, scale_idx).astype(h_ref.dtype)


def _normmod_router_kernel(x_ref, g_ref, mod_ref, rt_ref, h_ref, aff_ref, *, shift_idx, scale_idx):
    h = _normmod_body(x_ref, g_ref, mod_ref, shift_idx, scale_idx)
    h_ref[0] = h.astype(h_ref.dtype)
    logits = _dot3_nt(rt_ref[...], h)
    e = jnp.exp(logits - jnp.max(logits, axis=0, keepdims=True))
    aff_ref[0] = e * (1.0 / jnp.sum(e, axis=0, keepdims=True))


def _normmod(x, gain, mod, shift_idx, scale_idx, router_t=None):
    bsz, n, d = x.shape
    tr = _tile(n, 256)
    x_spec = pl.BlockSpec((1, tr, d), lambda b, t: (b, t, 0))
    g_spec = pl.BlockSpec((1, d), lambda b, t: (0, 0))
    m_spec = pl.BlockSpec((1, 8, d), lambda b, t: (b, 0, 0))
    if router_t is None:
        return pl.pallas_call(
            functools.partial(_normmod_kernel, shift_idx=shift_idx, scale_idx=scale_idx),
            grid=(bsz, n // tr), in_specs=[x_spec, g_spec, m_spec], out_specs=x_spec,
            out_shape=jax.ShapeDtypeStruct((bsz, n, d), BF16),
            compiler_params=_params(("parallel", "parallel")),
        )(x, gain, mod)
    ne = router_t.shape[0]
    return pl.pallas_call(
        functools.partial(_normmod_router_kernel, shift_idx=shift_idx, scale_idx=scale_idx),
        grid=(bsz, n // tr),
        in_specs=[x_spec, g_spec, m_spec, pl.BlockSpec((ne, d), lambda b, t: (0, 0))],
        out_specs=[x_spec, pl.BlockSpec((1, ne, tr), lambda b, t: (b, 0, t))],
        out_shape=[jax.ShapeDtypeStruct((bsz, n, d), BF16), jax.ShapeDtypeStruct((bsz, ne, n), F32)],
        compiler_params=_params(("parallel", "parallel")),
    )(x, gain, mod, router_t)


def _final_norm_kernel(x_ref, g_ref, o_ref):
    x = x_ref[0]
    o_ref[0] = x * lax.rsqrt(jnp.mean(x * x, axis=-1, keepdims=True) + EPS) * g_ref[...]


def _final_norm(x, gain):
    bsz, n, d = x.shape
    tr = _tile(n, 256)
    spec = pl.BlockSpec((1, tr, d), lambda b, t: (b, t, 0))
    return pl.pallas_call(
        _final_norm_kernel, grid=(bsz, n // tr),
        in_specs=[spec, pl.BlockSpec((1, d), lambda b, t: (0, 0))], out_specs=spec,
        out_shape=jax.ShapeDtypeStruct((bsz, n, d), F32),
        compiler_params=_params(("parallel", "parallel")),
    )(x, gain)


def _rope_tables(n):
    t = jnp.arange(n, dtype=jnp.int32)
    row = (t // GRID_W).astype(F32)
    col = (t % GRID_W).astype(F32)
    n_freq = A_DIM // 4
    inv = ROPE_BASE ** (-jnp.arange(n_freq, dtype=F32) / n_freq)
    ang = jnp.concatenate([row[:, None] * inv, col[:, None] * inv], axis=-1)
    cos, sin = jnp.cos(ang), jnp.sin(ang)
    return jnp.tile(cos, (1, 4)), jnp.concatenate([-sin, sin, -sin, sin], axis=-1)


def _rope(x, c, s):
    lane = lax.broadcasted_iota(jnp.int32, x.shape, 1)
    partner = jnp.where((lane % A_DIM) < A_DIM // 2,
                        pltpu.roll(x, LANES - A_DIM // 2, axis=1), pltpu.roll(x, A_DIM // 2, axis=1))
    return x * c + partner * s


def _attn_kernel(*refs, rope, with_ctx, lam_init):
    it = iter(refs)
    q_ref, k_ref, v_ref = next(it), next(it), next(it)
    if with_ctx:
        kc_ref, vc_ref = next(it), next(it)
    if rope:
        cq_ref, sq_ref, ck_ref, sk_ref = next(it), next(it), next(it), next(it)
    lam_ref, g_ref, o_ref = next(it), next(it), next(it)

    q = q_ref[0]
    k = k_ref[0]
    if rope:
        q = _rope(q, cq_ref[...], sq_ref[...])
        k = _rope(k, ck_ref[...], sk_ref[...])
    lane = lax.broadcasted_iota(jnp.int32, q.shape, 1)
    kb = k.astype(BF16)
    vb = v_ref[0].astype(BF16)
    if with_ctx:
        kcb = kc_ref[0].astype(BF16)
        vcb = vc_ref[0].astype(BF16)
    lp = lam_ref[...]
    lam = (jnp.exp(jnp.sum(lp[0:1] * lp[1:2], axis=1, keepdims=True))
           - jnp.exp(jnp.sum(lp[2:3] * lp[3:4], axis=1, keepdims=True)) + lam_init)
    scale = A_DIM ** -0.5

    def probs(half):
        qh = jnp.where((lane // A_DIM) == half, q, 0.0).astype(BF16)
        s = _dot_nt(qh, kb) * scale
        m = jnp.max(s, axis=-1, keepdims=True)
        if with_ctx:
            sc = _dot_nt(qh, kcb) * scale
            m = jnp.maximum(m, jnp.max(sc, axis=-1, keepdims=True))
            ec = jnp.exp(sc - m)
        e = jnp.exp(s - m)
        den = jnp.sum(e, axis=-1, keepdims=True)
        if with_ctx:
            den = den + jnp.sum(ec, axis=-1, keepdims=True)
        r = 1.0 / den
        return e * r, (ec * r if with_ctx else None)

    p0, pc0 = probs(0)
    p1, pc1 = probs(1)
    o = _dot((p0 - lam * p1).astype(BF16), vb)
    if with_ctx:
        o = o + _dot((pc0 - lam * pc1).astype(BF16), vcb)
    y = o * lax.rsqrt(jnp.mean(o * o, axis=-1, keepdims=True) + EPS) * g_ref[...]
    o_ref[0] = (y * (1.0 - lam_init)).astype(o_ref.dtype)


def _attention(pa, pa_ctx, lam_p, g_norm, lam_init, tables):
    bsz, n, _ = pa.shape
    tq = _tile(n, 256)
    h_off = A_HEADS
    in_specs = [
        pl.BlockSpec((1, tq, LANES), lambda b, h, i: (b, i, h)),
        pl.BlockSpec((1, n, LANES), lambda b, h, i: (b, 0, h_off + h)),
        pl.BlockSpec((1, n, LANES), lambda b, h, i: (b, 0, 2 * h_off + h)),
    ]
    args = [pa, pa, pa]
    with_ctx = pa_ctx is not None
    if with_ctx:
        nc = pa_ctx.shape[1]
        in_specs += [pl.BlockSpec((1, nc, LANES), lambda b, h, i: (b, 0, h_off + h)),
                     pl.BlockSpec((1, nc, LANES), lambda b, h, i: (b, 0, 2 * h_off + h))]
        args += [pa_ctx, pa_ctx]
    rope = tables is not None
    if rope:
        ct, st = tables
        in_specs += [pl.BlockSpec((tq, LANES), lambda b, h, i: (i, 0))] * 2
        in_specs += [pl.BlockSpec((n, LANES), lambda b, h, i: (0, 0))] * 2
        args += [ct, st, ct, st]
    in_specs += [pl.BlockSpec((4, A_DIM), lambda b, h, i: (0, 0)),
                 pl.BlockSpec((1, A_VDIM), lambda b, h, i: (0, 0))]
    args += [lam_p, g_norm]
    return pl.pallas_call(
        functools.partial(_attn_kernel, rope=rope, with_ctx=with_ctx, lam_init=lam_init),
        grid=(bsz, A_HEADS, n // tq), in_specs=in_specs,
        out_specs=pl.BlockSpec((1, tq, LANES), lambda b, h, i: (b, i, h)),
        out_shape=jax.ShapeDtypeStruct((bsz, n, A_HEADS * A_VDIM), BF16),
        compiler_params=_params(("parallel", "parallel", "parallel")),
    )(*args)


def _conv_specs(n, tr, tc, col_of):
    r = tr // HALO
    last = n // HALO - 1
    return [
        pl.BlockSpec((1, HALO, tc), lambda b, t, c: (b, jnp.maximum(t * r - 1, 0), col_of(c))),
        pl.BlockSpec((1, tr, tc), lambda b, t, c: (b, t, col_of(c))),
        pl.BlockSpec((1, HALO, tc), lambda b, t, c: (b, jnp.minimum((t + 1) * r, last), col_of(c))),
    ]


def _short_conv(prev_ref, x_ref, next_ref, w, taps):
    t = pl.program_id(1)
    nt = pl.num_programs(1)
    x = x_ref[0]
    tr = x.shape[0]
    prev = jnp.where(t > 0, prev_ref[0], 0.0)
    nxt = jnp.where(t < nt - 1, next_ref[0], 0.0)
    xe = jnp.concatenate([prev, x, nxt], axis=0)
    rows = tr + 2 * HALO
    pad = (taps - 1) // 2
    acc = None
    for j in range(taps):
        d = j - pad
        sh = xe if d == 0 else pltpu.roll(xe, (rows - d) % rows, axis=0)
        term = sh[HALO:HALO + tr] * w[j:j + 1]
        acc = term if acc is None else acc + term
    return acc


def _gdn_prep_kernel(prev_ref, x_ref, next_ref, w_ref, o_ref):
    y = _silu(_short_conv(prev_ref, x_ref, next_ref, w_ref[...], DN_CONV))
    c = pl.program_id(2)
    outs = []
    for h in range(DN_HEADS):
        t = y[:, h * DN_DK:(h + 1) * DN_DK]
        rs = lax.rsqrt(jnp.sum(t * t, axis=-1, keepdims=True) + EPS)
        f = jnp.where(c == 0, rs * (DN_DK ** -0.5), jnp.where(c == 1, rs, 1.0))
        outs.append(t * f)
    o_ref[0] = jnp.concatenate(outs, axis=1)


def _gdn_prep(pd, w_t):
    bsz, n, _ = pd.shape
    tr = _tile(n, 256)
    tc = DN_HEADS * DN_DK
    specs = _conv_specs(n, tr, tc, lambda c: c)
    return pl.pallas_call(
        _gdn_prep_kernel, grid=(bsz, n // tr, 3),
        in_specs=specs + [pl.BlockSpec((8, tc), lambda b, t, c: (0, c))],
        out_specs=pl.BlockSpec((1, tr, tc), lambda b, t, c: (b, t, c)),
        out_shape=jax.ShapeDtypeStruct((bsz, n, 3 * tc), F32),
        compiler_params=_params(("parallel", "parallel", "parallel")),
    )(pd, pd, pd, w_t)


def _gdn_gate_kernel(x_ref, alog_ref, dtb_ref, o_ref):
    x = x_ref[0]
    lane = lax.broadcasted_iota(jnp.int32, x.shape, 1)
    z = x + dtb_ref[...]
    softplus = jnp.maximum(z, 0.0) + jnp.log(1.0 + jnp.exp(-jnp.abs(z)))
    g = -jnp.exp(alog_ref[...]) * softplus
    o_ref[0] = jnp.where(lane < 2 * DN_HEADS, g, _sigmoid(x))


def _gdn_gates(pab, alog, dtb):
    bsz, n, w = pab.shape
    tr = _tile(n, 512)
    spec = pl.BlockSpec((1, tr, w), lambda b, t: (b, t, 0))
    row = pl.BlockSpec((1, w), lambda b, t: (0, 0))
    return pl.pallas_call(
        _gdn_gate_kernel, grid=(bsz, n // tr), in_specs=[spec, row, row], out_specs=spec,
        out_shape=jax.ShapeDtypeStruct((bsz, n, w), F32),
        compiler_params=_params(("parallel", "parallel")),
    )(pab, alog, dtb)


def _unit_tri_inverse(a):
    cs = a.shape[0]
    r = lax.broadcasted_iota(jnp.int32, (cs, cs), 0)
    c = lax.broadcasted_iota(jnp.int32, (cs, cs), 1)
    p = jnp.where(r == c, 1.0, 0.0) - a
    sq = a
    steps = int(math.log2(cs)) - 1
    for i in range(steps):
        sq = _dot3(sq, sq)
        p = p + _dot3(p, sq)
    return p


def _gdn_kernel(q_ref, k_ref, v_ref, kt_ref, gb_ref, gbt_ref, s0_ref, o_ref, st_ref):
    d = pl.program_id(1)
    step = pl.program_id(2)
    cs = DN_CHUNK
    nh = DN_HEADS

    @pl.when(step == 0)
    def _():
        st_ref[...] = s0_ref[...]

    r = lax.broadcasted_iota(jnp.int32, (cs, cs), 0)
    c = lax.broadcasted_iota(jnp.int32, (cs, cs), 1)
    fwd = d == 0
    lead = jnp.where(fwd, r - c, c - r)
    incl = lead >= 0
    strict = lead > 0
    incl_f = jnp.where(incl, 1.0, 0.0)
    gb = gb_ref[0]
    gbt = gbt_ref[0, 0]
    gcum_c = _dot3(incl_f, gb)
    gcum_r = _dot3(gbt, jnp.where(incl, 0.0, 1.0) + jnp.where(r == c, 1.0, 0.0))
    gc_all = jnp.where(fwd, gcum_c[:, 0:nh], gcum_c[:, nh:2 * nh])
    gr_all = jnp.where(fwd, gcum_r[0:nh, :], gcum_r[nh:2 * nh, :])
    b_all = jnp.where(fwd, gb[:, 2 * nh:3 * nh], gb[:, 3 * nh:4 * nh])
    q = q_ref[0]
    k = k_ref[0]
    v = v_ref[0]
    outs = []
    for h in range(nh):
        sl = slice(h * DN_DK, (h + 1) * DN_DK)
        qh, kh, vh = q[:, sl], k[:, sl], v[:, sl]
        kth = kt_ref[0, 0, h]
        gc = gc_all[:, h:h + 1]
        gr = gr_all[h:h + 1, :]
        bc = b_all[:, h:h + 1]
        glast = jnp.where(fwd, gc[cs - 1:cs, :], gc[0:1, :])
        decay = jnp.where(incl, jnp.exp(jnp.where(incl, gc - gr, 0.0)), 0.0)
        kb = kh.astype(BF16)
        ktb = kth.astype(BF16)
        kk = _dot(kb, ktb)
        a_mat = jnp.where(strict, bc * kk * decay, 0.0)
        t_mat = _unit_tri_inverse(a_mat).astype(BF16)
        egc = jnp.exp(gc)
        u = _dot(t_mat, (vh * bc).astype(BF16))
        w = _dot(t_mat, (kh * (bc * egc)).astype(BF16))
        qk = (_dot(qh.astype(BF16), ktb) * decay).astype(BF16)
        q_dec = (qh * egc).astype(BF16)
        kt_dec = (kth * jnp.exp(glast - gr)).astype(BF16)
        state = st_ref[0, 0, h]
        sb = state.astype(BF16)
        v_new = u - _dot(w.astype(BF16), sb)
        vnb = v_new.astype(BF16)
        outs.append(_dot(q_dec, sb) + _dot(qk, vnb))
        st_ref[0, 0, h] = state * jnp.exp(glast) + _dot(kt_dec, vnb)
    o_ref[0, 0] = jnp.concatenate(outs, axis=1)


def _gdn_scan(qkv, kt, gb, gbt, s0):
    bsz, n, _ = qkv.shape
    nc = n // DN_CHUNK
    w = DN_HEADS * DN_DK

    def cidx(d, s):
        return jnp.where(d == 0, s, nc - 1 - s)

    in_specs = [
        pl.BlockSpec((1, DN_CHUNK, w), lambda b, d, s: (b, cidx(d, s), 0)),
        pl.BlockSpec((1, DN_CHUNK, w), lambda b, d, s: (b, cidx(d, s), 1)),
        pl.BlockSpec((1, DN_CHUNK, w), lambda b, d, s: (b, cidx(d, s), 2)),
        pl.BlockSpec((1, 1, DN_HEADS, DN_DK, DN_CHUNK), lambda b, d, s: (b, cidx(d, s), 0, 0, 0)),
        pl.BlockSpec((1, DN_CHUNK, LANES), lambda b, d, s: (b, cidx(d, s), 0)),
        pl.BlockSpec((1, 1, LANES, DN_CHUNK), lambda b, d, s: (b, cidx(d, s), 0, 0)),
        pl.BlockSpec((1, 1, DN_HEADS, DN_DK, DN_DV), lambda b, d, s: (b, d, 0, 0, 0)),
    ]
    return pl.pallas_call(
        _gdn_kernel, grid=(bsz, 2, nc), in_specs=in_specs,
        out_specs=[pl.BlockSpec((1, 1, DN_CHUNK, w), lambda b, d, s: (d, b, cidx(d, s), 0)),
                   pl.BlockSpec((1, 1, DN_HEADS, DN_DK, DN_DV), lambda b, d, s: (b, d, 0, 0, 0))],
        out_shape=[jax.ShapeDtypeStruct((2, bsz, n, w), F32),
                   jax.ShapeDtypeStruct((bsz, 2, DN_HEADS, DN_DK, DN_DV), F32)],
        compiler_params=_params(("parallel", "parallel", "arbitrary")),
    )(qkv, qkv, qkv, kt, gb, gbt, s0)


def _gdn_post_kernel(o_ref, z_ref, g_ref, y_ref):
    o = o_ref[0, 0] + o_ref[1, 0]
    z = z_ref[0]
    outs = []
    for h in range(DN_HEADS):
        sl = slice(h * DN_DV, (h + 1) * DN_DV)
        t = o[:, sl]
        outs.append(t * lax.rsqrt(jnp.mean(t * t, axis=-1, keepdims=True) + EPS) * g_ref[...])
    y_ref[0] = (jnp.concatenate(outs, axis=1) * _silu(z)).astype(y_ref.dtype)


def _gdn_post(o, pd, g_norm):
    _, bsz, n, w = o.shape
    tr = _tile(n, 256)
    return pl.pallas_call(
        _gdn_post_kernel, grid=(bsz, n // tr),
        in_specs=[pl.BlockSpec((2, 1, tr, w), lambda b, t: (0, b, t, 0)),
                  pl.BlockSpec((1, tr, w), lambda b, t: (b, t, 3)),
                  pl.BlockSpec((1, DN_DV), lambda b, t: (0, 0))],
        out_specs=pl.BlockSpec((1, tr, w), lambda b, t: (b, t, 0)),
        out_shape=jax.ShapeDtypeStruct((bsz, n, w), BF16),
        compiler_params=_params(("parallel", "parallel")),
    )(o, pd, g_norm)


def _gdn_segment(pd, pab, conv_t, alog, dtb, s0):
    bsz, n, _ = pd.shape
    nc = n // DN_CHUNK
    qkv = _gdn_prep(pd, conv_t)
    gb = _gdn_gates(pab, alog, dtb)
    w = DN_HEADS * DN_DK
    kt = jnp.transpose(qkv[:, :, w:2 * w].reshape(bsz, nc, DN_CHUNK, DN_HEADS, DN_DK), (0, 1, 3, 4, 2))
    gbt = jnp.swapaxes(gb.reshape(bsz, nc, DN_CHUNK, LANES), 2, 3)
    return _gdn_scan(qkv, kt, gb, gbt, s0)


def _hy_prep_kernel(*refs):
    w_ref, u_ref, x0_ref = refs[9], refs[10], refs[11]
    w = w_ref[...]
    v = _short_conv(refs[0], refs[1], refs[2], w[0], HY_CONV)
    x1 = _short_conv(refs[3], refs[4], refs[5], w[1], HY_CONV)
    x0 = _short_conv(refs[6], refs[7], refs[8], w[2], HY_CONV)
    u_ref[0] = x1 * v
    x0_ref[0] = x0


def _hy_prep(ph, w3):
    bsz, n, _ = ph.shape
    tr = _tile(n, 256)
    tc = 512
    ncb = HY_WIDTH // tc
    specs = []
    for part in range(3):
        specs += _conv_specs(n, tr, tc, lambda c, part=part: part * ncb + c)
    out_spec = pl.BlockSpec((1, tr, tc), lambda b, t, c: (b, t, c))
    return pl.pallas_call(
        _hy_prep_kernel, grid=(bsz, n // tr, ncb),
        in_specs=specs + [pl.BlockSpec((3, 8, tc), lambda b, t, c: (0, 0, c))],
        out_specs=[out_spec, out_spec],
        out_shape=[jax.ShapeDtypeStruct((bsz, n, HY_WIDTH), F32)] * 2,
        compiler_params=_params(("parallel", "parallel", "parallel")),
    )(*([ph] * 9), w3)


def _dft_tables(n):
    big = 2 * n
    k = jnp.arange(n, dtype=jnp.int32)[:, None]
    t = jnp.arange(n, dtype=jnp.int32)[None, :]
    ang = ((k * t) % big).astype(F32) * (2.0 * math.pi / big)
    f_re = jnp.cos(ang)
    f_im = jnp.where(k == 0, (1 - 2 * (t % 2)).astype(F32), -jnp.sin(ang))
    return jnp.concatenate([f_re, f_im], axis=0)


def _colabs_kernel(c_ref, o_ref):
    o_ref[...] = jnp.sum(jnp.sum(jnp.abs(c_ref[...]), axis=0), axis=0, keepdims=True)


def _coef_kernel(fh_ref, nrm_ref, o_ref, *, n):
    tr = fh_ref.shape[2]
    kidx = pl.program_id(0) * tr + lax.broadcasted_iota(jnp.int32, fh_ref.shape[2:], 0)
    sgn = (1 - 2 * (kidx % 2)).astype(F32)
    first = kidx == 0
    fh = fh_ref[...]
    p = fh[0, 0] + sgn * fh[1, 0]
    qraw = fh[0, 1] + jnp.where(first, 1.0, sgn) * fh[1, 1]
    dk = jnp.where(first, 1.0 / (2 * n), 2.0 / (2 * n)) * (1.0 / nrm_ref[...])
    q_m = jnp.where(first, 0.0, qraw)
    o_ref[0] = dk * p
    o_ref[1] = -dk * q_m
    o_ref[2] = dk * q_m
    o_ref[3] = dk * jnp.where(first, qraw, p)


def _hy_filter_coefs(n, f_tab, w1p, b1, w2p, b2, f0, f1, w3, feat, tcol, rates2):
    h = _mm(feat[None], w1p[None], x3=True, extras=[(b1, "row"), (f0, "row")],
            epilogue=lambda acc, b, f: jnp.sin(f * (acc + b)))
    h = _mm(h, w2p[None], x3=True, extras=[(b2, "row"), (f1, "row")],
            epilogue=lambda acc, b, f: jnp.sin(f * (acc + b)))
    h = _mm(h, w3[None], x3=True, extras=[(tcol, "col"), (rates2, "row")],
            epilogue=lambda acc, t, rt: acc * jnp.exp(-t * rt))[0]
    c_lo = h[:, :HY_WIDTH]
    hb = h[:, HY_WIDTH:]
    c_hi = jnp.concatenate([jnp.zeros((1, HY_WIDTH), F32), hb[:0:-1]], axis=0)
    c2 = jnp.stack([c_lo, c_hi])
    tc = 256
    nrm = pl.pallas_call(
        _colabs_kernel, grid=(HY_WIDTH // tc,),
        in_specs=[pl.BlockSpec((2, n, tc), lambda j: (0, 0, j))],
        out_specs=pl.BlockSpec((1, tc), lambda j: (0, j)),
        out_shape=jax.ShapeDtypeStruct((1, HY_WIDTH), F32),
        compiler_params=_params(("parallel",)),
    )(c2)
    fh = _mm(f_tab[None], c2, x3=True).reshape(2, 2, n, HY_WIDTH)
    tr = _tile(n, 256)
    return pl.pallas_call(
        functools.partial(_coef_kernel, n=n), grid=(n // tr, HY_WIDTH // tc),
        in_specs=[pl.BlockSpec((2, 2, tr, tc), lambda t, j: (0, 0, t, j)),
                  pl.BlockSpec((1, tc), lambda t, j: (0, j))],
        out_specs=pl.BlockSpec((4, tr, tc), lambda t, j: (0, t, j)),
        out_shape=jax.ShapeDtypeStruct((4, n, HY_WIDTH), F32),
        compiler_params=_params(("parallel", "parallel")),
    )(fh, nrm)


def _freq_mul_kernel(x_ref, c_ref, y_ref):
    a = x_ref[0, 0]
    b = x_ref[0, 1]
    c = c_ref[...]
    y_ref[0, 0] = (a * c[0] + b * c[1]).astype(y_ref.dtype)
    y_ref[0, 1] = (a * c[2] + b * c[3]).astype(y_ref.dtype)


def _hyena(ph, conv3, coef, f_tab, ft_tab, d_skip):
    bsz, n, _ = ph.shape
    u, x0 = _hy_prep(ph, conv3)
    xf = _mm(f_tab[None], u).reshape(bsz, 2, n, HY_WIDTH)
    tr = _tile(n, 256)
    tc = 512
    spec = pl.BlockSpec((1, 2, tr, tc), lambda b, t, j: (b, 0, t, j))
    y = pl.pallas_call(
        _freq_mul_kernel, grid=(bsz, n // tr, HY_WIDTH // tc),
        in_specs=[spec, pl.BlockSpec((4, tr, tc), lambda b, t, j: (0, t, j))],
        out_specs=spec, out_shape=jax.ShapeDtypeStruct((bsz, 2, n, HY_WIDTH), BF16),
        compiler_params=_params(("parallel", "parallel", "parallel")),
    )(xf, coef)
    return _mm(ft_tab[None], y.reshape(bsz, 2 * n, HY_WIDTH), out_dtype=BF16,
               extras=[(u, "tile"), (x0, "tile"), (d_skip, "row")],
               epilogue=lambda acc, uu, xx, dd: xx * (acc + dd * uu))


def _merge_kernel(ya_ref, yd_ref, yh_ref, g0_ref, g1_ref, g2_ref, w_ref, o_ref):
    acc = _sigmoid(g0_ref[...]) * _dot(ya_ref[...], w_ref[0])
    acc = acc + _sigmoid(g1_ref[...]) * _dot(yd_ref[...], w_ref[1])
    acc = acc + _sigmoid(g2_ref[...]) * _dot(yh_ref[...], w_ref[2])
    o_ref[...] = acc.astype(o_ref.dtype)


def _merge(ya, yd, yh, pg, w_branch):
    m = ya.shape[0]
    tm = _tile(m, 512)
    tn = 512
    nj = D_MODEL // tn
    br = pl.BlockSpec((tm, ya.shape[1]), lambda j, i: (i, 0))
    gate = lambda r: pl.BlockSpec((tm, tn), lambda j, i, r=r: (i, r * nj + j))
    return pl.pallas_call(
        _merge_kernel, grid=(nj, m // tm),
        in_specs=[br, br, br, gate(0), gate(1), gate(2),
                  pl.BlockSpec((N_BRANCH, ya.shape[1], tn), lambda j, i: (0, 0, j))],
        out_specs=pl.BlockSpec((tm, tn), lambda j, i: (i, j)),
        out_shape=jax.ShapeDtypeStruct((m, D_MODEL), BF16),
        compiler_params=_params(("parallel", "parallel")),
    )(ya, yd, yh, pg, pg, pg, w_branch)


def _select_kernel(aff_ref, tri_ref, slot_ref, *, cap):
    aff = aff_ref[0]
    bits = pltpu.bitcast(aff, jnp.int32)
    thr = jnp.zeros((aff.shape[0], 1), jnp.int32)
    for bit in range(30, -1, -1):
        cand = thr | (1 << bit)
        cnt = jnp.sum(jnp.where(bits >= cand, 1.0, 0.0), axis=1, keepdims=True)
        thr = jnp.where(cnt >= cap, cand, thr)
    gt = bits > thr
    eq = bits == thr
    need = cap - jnp.sum(jnp.where(gt, 1.0, 0.0), axis=1, keepdims=True)
    tri = tri_ref[...]
    eq_before = _dot(jnp.where(eq, 1.0, 0.0).astype(BF16), tri)
    sel = gt | (eq & (eq_before < need))
    slot = _dot(jnp.where(sel, 1.0, 0.0).astype(BF16), tri)
    slot_ref[0] = jnp.where(sel, slot, -1.0).astype(jnp.int32)


def _select(aff_t, tri, cap):
    bsz, ne, n = aff_t.shape
    return pl.pallas_call(
        functools.partial(_select_kernel, cap=cap), grid=(bsz,),
        in_specs=[pl.BlockSpec((1, ne, n), lambda b: (b, 0, 0)), pl.BlockSpec((n, n), lambda b: (0, 0))],
        out_specs=pl.BlockSpec((1, ne, n), lambda b: (b, 0, 0)),
        out_shape=jax.ShapeDtypeStruct((bsz, ne, n), jnp.int32),
        compiler_params=_params(("parallel",)),
    )(aff_t, tri)


def _gather_kernel(slot_ref, aff_ref, h_ref, xe_ref, gs_ref, *, cap):
    e = pl.program_id(1)
    slot = slot_ref[0, pl.ds(e, 1), :]
    aff = aff_ref[0, pl.ds(e, 1), :]
    n = slot.shape[1]
    hit = lax.broadcasted_iota(jnp.int32, (cap, n), 0) == slot
    xe_ref[0, 0] = _dot(jnp.where(hit, 1.0, 0.0).astype(BF16), h_ref[0]).astype(xe_ref.dtype)
    gs_ref[0, 0] = jnp.sum(jnp.where(hit, aff, 0.0), axis=1, keepdims=True)


def _gather(slot, aff_t, h, cap):
    bsz, ne, n = slot.shape
    d = h.shape[2]
    row = pl.BlockSpec((1, ne, n), lambda b, e: (b, 0, 0))
    return pl.pallas_call(
        functools.partial(_gather_kernel, cap=cap), grid=(bsz, ne),
        in_specs=[row, row, pl.BlockSpec((1, n, d), lambda b, e: (b, 0, 0))],
        out_specs=[pl.BlockSpec((1, 1, cap, d), lambda b, e: (e, b, 0, 0)),
                   pl.BlockSpec((1, 1, cap, 1), lambda b, e: (e, b, 0, 0))],
        out_shape=[jax.ShapeDtypeStruct((ne, bsz, cap, d), BF16),
                   jax.ShapeDtypeStruct((ne, bsz, cap, 1), F32)],
        compiler_params=_params(("parallel", "arbitrary")),
    )(slot, aff_t, h)


def _expert_kernel(x_ref, gs_ref, wg_ref, wu_ref, wd_ref, y_ref):
    x = x_ref[0]
    a = _dot(x, wg_ref[0])
    u = _dot(x, wu_ref[0])
    y = _dot((_silu(a) * u).astype(BF16), wd_ref[0])
    y_ref[0] = (y * gs_ref[0]).astype(y_ref.dtype)


def _experts(xe, gs, wg, wu, wd):
    ne, m, d = xe.shape
    ff = wg.shape[2]
    tm = _tile(m, 512)
    return pl.pallas_call(
        _expert_kernel, grid=(ne, m // tm),
        in_specs=[pl.BlockSpec((1, tm, d), lambda e, i: (e, i, 0)),
                  pl.BlockSpec((1, tm, 1), lambda e, i: (e, i, 0)),
                  pl.BlockSpec((1, d, ff), lambda e, i: (e, 0, 0)),
                  pl.BlockSpec((1, d, ff), lambda e, i: (e, 0, 0)),
                  pl.BlockSpec((1, ff, d), lambda e, i: (e, 0, 0))],
        out_specs=pl.BlockSpec((1, tm, d), lambda e, i: (e, i, 0)),
        out_shape=jax.ShapeDtypeStruct((ne, m, d), BF16),
        compiler_params=_params(("parallel", "parallel")),
    )(xe, gs, wg, wu, wd)


def _scatter_kernel(slot_ref, ye_ref, x_ref, g_ref, o_ref, acc_ref, *, cap):
    e = pl.program_id(2)
    ne = pl.num_programs(2)
    slot = slot_ref[0, 0]
    n = slot.shape[0]
    hit = lax.broadcasted_iota(jnp.int32, (n, cap), 1) == slot
    part = _dot(jnp.where(hit, 1.0, 0.0).astype(BF16), ye_ref[0, 0])

    @pl.when(e == 0)
    def _():
        acc_ref[...] = part

    @pl.when(e > 0)
    def _():
        acc_ref[...] += part

    @pl.when(e == ne - 1)
    def _():
        o_ref[0] = x_ref[0] + g_ref[0] * acc_ref[...]


def _scatter_residual(slot_col, ye, x, gate_row, cap):
    bsz, ne, n, _ = slot_col.shape
    d = x.shape[2]
    tn = 512
    xs = pl.BlockSpec((1, n, tn), lambda b, j, e: (b, 0, j))
    return pl.pallas_call(
        functools.partial(_scatter_kernel, cap=cap), grid=(bsz, d // tn, ne),
        in_specs=[pl.BlockSpec((1, 1, n, 1), lambda b, j, e: (b, e, 0, 0)),
                  pl.BlockSpec((1, 1, cap, tn), lambda b, j, e: (e, b, 0, j)),
                  xs, pl.BlockSpec((1, 1, tn), lambda b, j, e: (b, 0, j))],
        out_specs=xs, out_shape=jax.ShapeDtypeStruct((bsz, n, d), F32),
        scratch_shapes=[pltpu.VMEM((n, tn), F32)],
        compiler_params=_params(("parallel", "parallel", "arbitrary")),
    )(slot_col, ye, x, gate_row)


def _ffn(x, gain, mod, router_t, wg, wu, wd, tri):
    bsz, n, d = x.shape
    cap = EC_CAPACITY * n // N_EXPERTS
    h, aff_t = _normmod(x, gain, mod, 3, 4, router_t)
    slot = _select(aff_t, tri, cap)
    xe, gs = _gather(slot, aff_t, h, cap)
    ye = _experts(xe.reshape(N_EXPERTS, bsz * cap, d), gs.reshape(N_EXPERTS, bsz * cap, 1), wg, wu, wd)
    return _scatter_residual(slot[..., None], ye.reshape(N_EXPERTS, bsz, cap, d), x, mod[:, 5:6], cap)


def _hy_consts(n):
    t = jnp.linspace(0.0, 1.0, n, dtype=F32)[:, None]
    pos = jnp.arange(n, dtype=F32)
    bands = jnp.linspace(1e-4, HY_BANDS - 1, HY_BANDS, dtype=F32)
    ang = (2.0 * math.pi / n) * pos[:, None] * bands[None, :]
    feat = jnp.concatenate([t, jnp.cos(ang), -jnp.sin(ang)], axis=-1)
    feat = jnp.pad(feat, ((0, 0), (0, LANES - HY_EMB)))
    f_tab = _dft_tables(n)
    return dict(feat=feat, tcol=t[None], f_tab=f_tab, f_bf=f_tab.astype(BF16),
                ft_bf=jnp.transpose(f_tab).astype(BF16))


def _mixer_segment_proj(h, w_attn, w_dn, w_hy, w_gate, w_ab):
    bsz, n, d = h.shape
    hf = h.reshape(1, bsz * n, d)
    proj = lambda w, tn: _mm(hf, w[None], tm=1024, tn=tn).reshape(bsz, n, w.shape[1])
    return proj(w_attn, 1024), proj(w_dn, 1024), proj(w_hy, 1024), proj(w_gate, 1024), proj(w_ab, 128)


def kernel(x, c, ctx, c_ctx, w_ada, b_ada, norm_mix, norm_ffn, w_in, diff_lambda, diff_norm, dn_conv,
           dn_a_log, dn_dt_bias, dn_norm, hy_conv, hy_w1, hy_b1, hy_w2, hy_b2, hy_freq, hy_w3, hy_bias,
           w_branch, w_out, router, w_gate, w_up, w_down, norm_final):
    depth = w_ada.shape[0]
    bsz, n_lat, d = x.shape
    n_ctx = ctx.shape[1]
    assert d == D_MODEL and n_lat % 256 == 0 and n_ctx % DN_CHUNK == 0

    rope_tabs = _rope_tables(n_lat)
    hy_l = _hy_consts(n_lat)
    hy_c = _hy_consts(n_ctx)
    tri_l = (jnp.arange(n_lat)[:, None] < jnp.arange(n_lat)[None, :]).astype(BF16)
    tri_c = (jnp.arange(n_ctx)[:, None] < jnp.arange(n_ctx)[None, :]).astype(BF16)
    rates = jnp.abs(jnp.linspace(HY_DECAY_MIN, HY_DECAY_MAX, HY_WIDTH, dtype=F32))
    rates2 = jnp.concatenate([rates, rates])[None, None]
    cc = jnp.concatenate([c, jnp.broadcast_to(c_ctx[None], (8, d))], axis=0)[None]

    o_aq, o_dq, o_da, o_hy, o_gt = 0, 3072, 7168, 7200, 10272

    for l in range(depth):
        with_ctx = l < depth - 1
        lam_init = 0.8 - 0.6 * math.exp(-0.3 * l)
        mod = _mm(cc, w_ada[l][None], tn=1024, prologue=_silu, extras=[(b_ada[l][None, None], "row")],
                  epilogue=lambda acc, bias: acc + bias)[0]
        mod_l = jnp.pad(mod[:bsz].reshape(bsz, 6, d), ((0, 0), (0, 2), (0, 0)))
        mod_c = jnp.broadcast_to(jnp.pad(mod[bsz].reshape(1, 6, d), ((0, 0), (0, 2), (0, 0))), (bsz, 8, d))

        wl = w_in[l]
        w_attn = wl[:, o_aq:o_dq].astype(BF16)
        w_dn = wl[:, o_dq:o_da].astype(BF16)
        w_ab = jnp.pad(wl[:, o_da:o_hy], ((0, 0), (0, LANES - 4 * DN_HEADS))).astype(BF16)
        w_hy = wl[:, o_hy:o_gt].astype(BF16)
        w_gt = wl[:, o_gt:].astype(BF16)
        wb = w_branch[l].astype(BF16)
        wo = w_out[l].astype(BF16)
        wg, wu, wd = w_gate[l].astype(BF16), w_up[l].astype(BF16), w_down[l].astype(BF16)
        router_t = jnp.transpose(router[l])
        gain_mix, gain_ffn = norm_mix[l][None], norm_ffn[l][None]
        conv_dn = jnp.pad(jnp.transpose(dn_conv[l]), ((0, 8 - DN_CONV), (0, 0)))
        conv_hy = jnp.pad(jnp.transpose(hy_conv[l]), ((0, 8 - HY_CONV), (0, 0)))
        conv_hy = jnp.transpose(conv_hy.reshape(8, 3, HY_WIDTH), (1, 0, 2))
        alog = jnp.pad(dn_a_log[l].reshape(1, -1), ((0, 0), (0, LANES - 2 * DN_HEADS)))
        dtb = jnp.pad(dn_dt_bias[l].reshape(1, -1), ((0, 0), (0, LANES - 2 * DN_HEADS)))
        pad_h = lambda w: jnp.pad(w, ((0, LANES - w.shape[0]), (0, LANES - w.shape[1])))
        pad_r = lambda v: jnp.pad(v, (0, LANES - v.shape[0]))[None, None]
        w1p = pad_h(hy_w1[l])
        w2p = pad_h(hy_w2[l])
        w3p = jnp.pad(hy_w3[l], ((0, LANES - HY_HIDDEN), (0, 0)))
        filt = lambda n, hc: _hy_filter_coefs(
            n, hc["f_tab"], w1p, pad_r(hy_b1[l]), w2p, pad_r(hy_b2[l]), pad_r(hy_freq[l, 0]),
            pad_r(hy_freq[l, 1]), w3p, hc["feat"], hc["tcol"], rates2)
        d_skip = hy_bias[l][None, None]
        g_dn = dn_norm[l][None]
        g_diff = diff_norm[l][None]

        h_l = _normmod(x, gain_mix, mod_l, 0, 1)
        h_c = _normmod(ctx, gain_mix, mod_c, 0, 1)
        pa_l, pd_l, ph_l, pg_l, pab_l = _mixer_segment_proj(h_l, w_attn, w_dn, w_hy, w_gt, w_ab)
        pa_c, pd_c, ph_c, pg_c, pab_c = _mixer_segment_proj(h_c, w_attn, w_dn, w_hy, w_gt, w_ab)

        ya_l = _attention(pa_l, pa_c, diff_lambda[l], g_diff, lam_init, rope_tabs)
        s0 = jnp.zeros((bsz, 2, DN_HEADS, DN_DK, DN_DV), F32)
        o_c, s_c = _gdn_segment(pd_c, pab_c, conv_dn, alog, dtb, s0)
        o_l, _ = _gdn_segment(pd_l, pab_l, conv_dn, alog, dtb, s_c)
        yd_l = _gdn_post(o_l, pd_l, g_dn)
        yh_l = _hyena(ph_l, conv_hy, filt(n_lat, hy_l), hy_l["f_bf"], hy_l["ft_bf"], d_skip)

        def merge_out(ya, yd, yh, pg, xs, mod_s):
            n = xs.shape[1]
            flat = lambda t: t.reshape(bsz * n, t.shape[2])
            y = _merge(flat(ya), flat(yd), flat(yh), flat(pg), wb).reshape(bsz, n, d)
            return _mm(y, wo[None], tm=1024, tn=1024, extras=[(xs, "tile"), (mod_s[:, 2:3], "row")],
                       epilogue=lambda acc, xx, gg: xx + gg * acc)

        x = merge_out(ya_l, yd_l, yh_l, pg_l, x, mod_l)
        x = _ffn(x, gain_ffn, mod_l, router_t, wg, wu, wd, tri_l)
        if with_ctx:
            ya_c = _attention(pa_c, None, diff_lambda[l], g_diff, lam_init, None)
            yd_c = _gdn_post(o_c, pd_c, g_dn)
            yh_c = _hyena(ph_c, conv_hy, filt(n_ctx, hy_c), hy_c["f_bf"], hy_c["ft_bf"], d_skip)
            ctx = merge_out(ya_c, yd_c, yh_c, pg_c, ctx, mod_c)
            ctx = _ffn(ctx, gain_ffn, mod_c, router_t, wg, wu, wd, tri_c)
    return _final_norm(x, norm_final[None])
```

```python
import functools
import math

import jax
import jax.numpy as jnp
from jax import lax
from jax.experimental import pallas as pl
from jax.experimental.pallas import tpu as pltpu

F32 = jnp.float32
BF16 = jnp.bfloat16

D_MODEL = 2048
GRID_W = 64
EPS = 1e-6
A_HEADS = 8
A_DIM = 64
A_VDIM = 128
ROPE_BASE = 10000.0
DN_HEADS = 8
DN_DK = 128
DN_DV = 128
DN_CONV = 5
DN_CHUNK = 64
HY_WIDTH = 1024
HY_CONV = 3
HY_BANDS = 16
HY_EMB = 1 + 2 * HY_BANDS
HY_HIDDEN = 64
HY_DECAY_MIN = math.log(1e-2) / 1.5
HY_DECAY_MAX = math.log(1e-2) / 0.3
N_BRANCH = 3
N_EXPERTS = 16
EXPERT_FF = 1024
EC_CAPACITY = 2

LANES = 128
SUBLANES = 8
HALO = SUBLANES
VMEM_LIMIT = 56 * 1024 * 1024


def _params(sem, vmem=VMEM_LIMIT):
    return pltpu.CompilerParams(dimension_semantics=sem, vmem_limit_bytes=vmem)


def _tile(n, pref):
    if n <= pref:
        return n
    t = pref
    while n % t:
        t -= SUBLANES
    return t


def _split3(x):
    hi = x.astype(BF16)
    lo = (x - hi.astype(F32)).astype(BF16)
    return hi, lo


def _dot(a, b):
    return jnp.dot(a, b, preferred_element_type=F32)


def _dot3(a, b):
    ah, al = _split3(a)
    bh, bl = _split3(b)
    return _dot(ah, bh) + (_dot(ah, bl) + _dot(al, bh))


def _dot_nt(a, b):
    return lax.dot_general(a, b, (((1,), (1,)), ((), ())), preferred_element_type=F32)


def _dot3_nt(a, b):
    ah, al = _split3(a)
    bh, bl = _split3(b)
    return _dot_nt(ah, bh) + (_dot_nt(ah, bl) + _dot_nt(al, bh))


def _silu(x):
    return x * (1.0 / (1.0 + jnp.exp(-x)))


def _sigmoid(x):
    return 1.0 / (1.0 + jnp.exp(-x))


def _mm_kernel(*refs, nk, n_extra, epilogue, prologue, x3):
    a_ref, b_ref = refs[0], refs[1]
    extra_refs = refs[2:2 + n_extra]
    o_ref = refs[2 + n_extra]
    a = a_ref[0]
    b = b_ref[0]
    if prologue is not None:
        a = prologue(a)
    if x3:
        part = _dot3(a.astype(F32), b.astype(F32))
    else:
        part = _dot(a.astype(BF16), b.astype(BF16))

    def finish(acc):
        if epilogue is not None:
            acc = epilogue(acc, *[r[0] for r in extra_refs])
        o_ref[0] = acc.astype(o_ref.dtype)

    if nk == 1:
        finish(part)
    else:
        acc_ref = refs[3 + n_extra]
        k = pl.program_id(3)

        @pl.when(k == 0)
        def _():
            acc_ref[...] = part

        @pl.when(k > 0)
        def _():
            acc_ref[...] += part

        @pl.when(k == nk - 1)
        def _():
            finish(acc_ref[...])


def _mm(a, b, *, out_dtype=F32, tm=512, tn=512, tk=None, extras=(), epilogue=None, prologue=None,
        x3=False, name="matmul"):
    ga, m, k = a.shape
    gb, kb, n = b.shape
    assert k == kb
    g = max(ga, gb)
    tm = _tile(m, tm)
    tn = n if n <= tn else tn
    tk = k if tk is None else min(tk, k)
    assert m % tm == 0 and n % tn == 0 and k % tk == 0, (m, n, k, tm, tn, tk)
    ni, nj, nk = m // tm, n // tn, k // tk
    in_specs = [
        pl.BlockSpec((1, tm, tk), lambda gi, j, i, kk: (gi if ga > 1 else 0, i, kk)),
        pl.BlockSpec((1, tk, tn), lambda gi, j, i, kk: (gi if gb > 1 else 0, kk, j)),
    ]
    args = [a, b]
    for arr, mode in extras:
        ge = arr.shape[0]
        if callable(mode):
            spec = mode(tm, tn)
        elif mode == "tile":
            spec = pl.BlockSpec((1, tm, tn), lambda gi, j, i, kk, ge=ge: (gi if ge > 1 else 0, i, j))
        elif mode == "row":
            spec = pl.BlockSpec((1, 1, tn), lambda gi, j, i, kk, ge=ge: (gi if ge > 1 else 0, 0, j))
        else:
            spec = pl.BlockSpec((1, tm, 1), lambda gi, j, i, kk, ge=ge: (gi if ge > 1 else 0, i, 0))
        in_specs.append(spec)
        args.append(arr)
    scratch = [pltpu.VMEM((tm, tn), F32)] if nk > 1 else []
    return pl.pallas_call(
        functools.partial(_mm_kernel, nk=nk, n_extra=len(extras), epilogue=epilogue,
                          prologue=prologue, x3=x3),
        grid=(g, nj, ni, nk),
        in_specs=in_specs,
        out_specs=pl.BlockSpec((1, tm, tn), lambda gi, j, i, kk: (gi, i, j)),
        out_shape=jax.ShapeDtypeStruct((g, m, n), out_dtype),
        scratch_shapes=scratch, name=name,
        compiler_params=_params(("parallel", "parallel", "parallel", "arbitrary")),
    )(*args)


def _normmod_body(x_ref, g_ref, mod_ref, shift_idx, scale_idx):
    x = x_ref[0]
    y = x * lax.rsqrt(jnp.mean(x * x, axis=-1, keepdims=True) + EPS) * g_ref[...]
    m = mod_ref[0]
    return y * (1.0 + m[scale_idx:scale_idx + 1]) + m[shift_idx:shift_idx + 1]


def _normmod_kernel(x_ref, g_ref, mod_ref, h_ref, *, shift_idx, scale_idx):
    h_ref[0] = _normmod_body(x_ref, g_ref, mod_ref, shift_idx, scale_idx).astype(h_ref.dtype)


def _normmod_router_kernel(x_ref, g_ref, mod_ref, rt_ref, h_ref, aff_ref, *, shift_idx, scale_idx):
    h = _normmod_body(x_ref, g_ref, mod_ref, shift_idx, scale_idx)
    h_ref[0] = h.astype(h_ref.dtype)
    logits = _dot3_nt(rt_ref[...], h)
    e = jnp.exp(logits - jnp.max(logits, axis=0, keepdims=True))
    aff_ref[0] = e * (1.0 / jnp.sum(e, axis=0, keepdims=True))


def _normmod(x, gain, mod, shift_idx, scale_idx, router_t=None):
    bsz, n, d = x.shape
    tr = _tile(n, 256)
    x_spec = pl.BlockSpec((1, tr, d), lambda b, t: (b, t, 0))
    g_spec = pl.BlockSpec((1, d), lambda b, t: (0, 0))
    m_spec = pl.BlockSpec((1, 8, d), lambda b, t: (b, 0, 0))
    if router_t is None:
        return pl.pallas_call(
            functools.partial(_normmod_kernel, shift_idx=shift_idx, scale_idx=scale_idx),
            grid=(bsz, n // tr), in_specs=[x_spec, g_spec, m_spec], out_specs=x_spec,
            out_shape=jax.ShapeDtypeStruct((bsz, n, d), BF16),
            name="normmod", compiler_params=_params(("parallel", "parallel")),
        )(x, gain, mod)
    ne = router_t.shape[0]
    return pl.pallas_call(
        functools.partial(_normmod_router_kernel, shift_idx=shift_idx, scale_idx=scale_idx),
        grid=(bsz, n // tr),
        in_specs=[x_spec, g_spec, m_spec, pl.BlockSpec((ne, d), lambda b, t: (0, 0))],
        out_specs=[x_spec, pl.BlockSpec((1, ne, tr), lambda b, t: (b, 0, t))],
        out_shape=[jax.ShapeDtypeStruct((bsz, n, d), BF16), jax.ShapeDtypeStruct((bsz, ne, n), F32)],
        name="normmod_router", compiler_params=_params(("parallel", "parallel")),
    )(x, gain, mod, router_t)


def _final_norm_kernel(x_ref, g_ref, o_ref):
    x = x_ref[0]
    o_ref[0] = x * lax.rsqrt(jnp.mean(x * x, axis=-1, keepdims=True) + EPS) * g_ref[...]


def _final_norm(x, gain):
    bsz, n, d = x.shape
    tr = _tile(n, 256)
    spec = pl.BlockSpec((1, tr, d), lambda b, t: (b, t, 0))
    return pl.pallas_call(
        _final_norm_kernel, grid=(bsz, n // tr), name="final_norm",
        in_specs=[spec, pl.BlockSpec((1, d), lambda b, t: (0, 0))], out_specs=spec,
        out_shape=jax.ShapeDtypeStruct((bsz, n, d), F32),
        compiler_params=_params(("parallel", "parallel")),
    )(x, gain)


def _rope_tables(n):
    t = jnp.arange(n, dtype=jnp.int32)
    row = (t // GRID_W).astype(F32)
    col = (t % GRID_W).astype(F32)
    n_freq = A_DIM // 4
    inv = ROPE_BASE ** (-jnp.arange(n_freq, dtype=F32) / n_freq)
    ang = jnp.concatenate([row[:, None] * inv, col[:, None] * inv], axis=-1)
    cos, sin = jnp.cos(ang), jnp.sin(ang)
    return jnp.tile(cos, (1, 4)), jnp.concatenate([-sin, sin, -sin, sin], axis=-1)


def _rope(x, c, s):
    lane = lax.broadcasted_iota(jnp.int32, x.shape, 1)
    partner = jnp.where((lane % A_DIM) < A_DIM // 2,
                        pltpu.roll(x, LANES - A_DIM // 2, axis=1), pltpu.roll(x, A_DIM // 2, axis=1))
    return x * c + partner * s


ATTN_Q_TILE = 1024
ATTN_Q_SUB = 256


def _attn_kernel(*refs, rope, with_ctx):
    it = iter(refs)
    q_ref, k_ref, v_ref = next(it), next(it), next(it)
    if with_ctx:
        kc_ref, vc_ref = next(it), next(it)
    if rope:
        cq_ref, sq_ref, ck_ref, sk_ref = next(it), next(it), next(it), next(it)
    lam_ref, lam0_ref, g_ref, o_ref = next(it), next(it), next(it), next(it)
    kb_s, va_s = next(it), next(it)
    if with_ctx:
        kcb_s, vca_s = next(it), next(it)

    @pl.when(pl.program_id(2) == 0)
    def _():
        k = k_ref[0]
        if rope:
            k = _rope(k, ck_ref[...], sk_ref[...])
        kb_s[...] = k.astype(BF16)
        va_s[:, :A_VDIM] = v_ref[0].astype(BF16)
        va_s[:, A_VDIM:] = jnp.ones((va_s.shape[0], A_VDIM), BF16)
        if with_ctx:
            kcb_s[...] = kc_ref[0].astype(BF16)
            vca_s[:, :A_VDIM] = vc_ref[0].astype(BF16)
            vca_s[:, A_VDIM:] = jnp.ones((vca_s.shape[0], A_VDIM), BF16)

    q = q_ref[0]
    if rope:
        q = _rope(q, cq_ref[...], sq_ref[...])
    q = q * (A_DIM ** -0.5 * math.log2(math.e))
    lane = lax.broadcasted_iota(jnp.int32, q.shape, 1)
    lp = lam_ref[...]
    lam0 = lam0_ref[...]
    lam = (jnp.exp(jnp.sum(lp[0:1] * lp[1:2], axis=1, keepdims=True))
           - jnp.exp(jnp.sum(lp[2:3] * lp[3:4], axis=1, keepdims=True)) + lam0)
    kb = kb_s[...]
    tq = q.shape[0]
    sub = ATTN_Q_SUB if tq % ATTN_Q_SUB == 0 else tq
    chains = [(r0, half) for r0 in range(0, tq, sub) for half in (0, 1)]
    qhs = [jnp.where((lane // A_DIM) == half, q, 0.0).astype(BF16) for half in (0, 1)]
    st = {}

    def scores(c):
        qh = qhs[c[1]][c[0]:c[0] + sub]
        st[c] = (_dot_nt(qh, kb), _dot_nt(qh, kcb_s[...]) if with_ctx else None)

    def softmax_pv(c):
        s, sc = st[c]
        m = jnp.max(s, axis=-1, keepdims=True)
        if with_ctx:
            m = jnp.maximum(m, jnp.max(sc, axis=-1, keepdims=True))
        r = _dot(jnp.exp2(s - m).astype(BF16), va_s[...])
        if with_ctx:
            r = r + _dot(jnp.exp2(sc - m).astype(BF16), vca_s[...])
        st[c] = r[:, :A_VDIM] * (1.0 / r[:, A_VDIM:A_VDIM + 1])

    scores(chains[0])
    for j, c in enumerate(chains):
        if j + 1 < len(chains):
            scores(chains[j + 1])
        softmax_pv(c)
    rows = [st[(r0, 0)] - lam * st[(r0, 1)] for r0 in range(0, tq, sub)]
    o = jnp.concatenate(rows, axis=0) if len(rows) > 1 else rows[0]
    y = o * lax.rsqrt(jnp.mean(o * o, axis=-1, keepdims=True) + EPS) * g_ref[...]
    o_ref[0] = (y * (1.0 - lam0)).astype(o_ref.dtype)


def _attention(pa, pa_ctx, lam_p, g_norm, lam0, tables):
    bsz, n, _ = pa.shape
    tq = _tile(n, ATTN_Q_TILE)
    h_off = A_HEADS
    in_specs = [
        pl.BlockSpec((1, tq, LANES), lambda b, h, i: (b, i, h)),
        pl.BlockSpec((1, n, LANES), lambda b, h, i: (b, 0, h_off + h)),
        pl.BlockSpec((1, n, LANES), lambda b, h, i: (b, 0, 2 * h_off + h)),
    ]
    args = [pa, pa, pa]
    scratch = [pltpu.VMEM((n, LANES), BF16), pltpu.VMEM((n, 2 * A_VDIM), BF16)]
    with_ctx = pa_ctx is not None
    if with_ctx:
        nc = pa_ctx.shape[1]
        in_specs += [pl.BlockSpec((1, nc, LANES), lambda b, h, i: (b, 0, h_off + h)),
                     pl.BlockSpec((1, nc, LANES), lambda b, h, i: (b, 0, 2 * h_off + h))]
        args += [pa_ctx, pa_ctx]
        scratch += [pltpu.VMEM((nc, LANES), BF16), pltpu.VMEM((nc, 2 * A_VDIM), BF16)]
    rope = tables is not None
    if rope:
        ct, st = tables
        in_specs += [pl.BlockSpec((tq, LANES), lambda b, h, i: (i, 0))] * 2
        in_specs += [pl.BlockSpec((n, LANES), lambda b, h, i: (0, 0))] * 2
        args += [ct, st, ct, st]
    in_specs += [pl.BlockSpec((4, A_DIM), lambda b, h, i: (0, 0)),
                 pl.BlockSpec((1, 1), lambda b, h, i: (0, 0)),
                 pl.BlockSpec((1, A_VDIM), lambda b, h, i: (0, 0))]
    args += [lam_p, lam0, g_norm]
    return pl.pallas_call(
        functools.partial(_attn_kernel, rope=rope, with_ctx=with_ctx),
        grid=(bsz, A_HEADS, n // tq), in_specs=in_specs,
        out_specs=pl.BlockSpec((1, tq, LANES), lambda b, h, i: (b, i, h)),
        out_shape=jax.ShapeDtypeStruct((bsz, n, A_HEADS * A_VDIM), BF16),
        scratch_shapes=scratch, name="diff_attention",
        compiler_params=_params(("parallel", "parallel", "arbitrary")),
    )(*args)


def _conv_specs(n, tr, tc, col_of):
    r = tr // HALO
    last = n // HALO - 1
    return [
        pl.BlockSpec((1, HALO, tc), lambda b, t, c: (b, jnp.maximum(t * r - 1, 0), col_of(c))),
        pl.BlockSpec((1, tr, tc), lambda b, t, c: (b, t, col_of(c))),
        pl.BlockSpec((1, HALO, tc), lambda b, t, c: (b, jnp.minimum((t + 1) * r, last), col_of(c))),
    ]


def _short_conv(prev_ref, x_ref, next_ref, w, taps):
    t = pl.program_id(1)
    nt = pl.num_programs(1)
    x = x_ref[0]
    tr = x.shape[0]
    prev = jnp.where(t > 0, prev_ref[0], 0.0)
    nxt = jnp.where(t < nt - 1, next_ref[0], 0.0)
    xe = jnp.concatenate([prev, x, nxt], axis=0)
    rows = tr + 2 * HALO
    pad = (taps - 1) // 2
    acc = None
    for j in range(taps):
        d = j - pad
        sh = xe if d == 0 else pltpu.roll(xe, (rows - d) % rows, axis=0)
        term = sh[HALO:HALO + tr] * w[j:j + 1]
        acc = term if acc is None else acc + term
    return acc


def _gdn_prep_kernel(prev_ref, x_ref, next_ref, w_ref, o_ref):
    y = _silu(_short_conv(prev_ref, x_ref, next_ref, w_ref[...], DN_CONV))
    c = pl.program_id(2)
    outs = []
    for h in range(DN_HEADS):
        t = y[:, h * DN_DK:(h + 1) * DN_DK]
        rs = lax.rsqrt(jnp.sum(t * t, axis=-1, keepdims=True) + EPS)
        f = jnp.where(c == 0, rs * (DN_DK ** -0.5), jnp.where(c == 1, rs, 1.0))
        outs.append(t * f)
    o_ref[0] = jnp.concatenate(outs, axis=1)


def _gdn_prep(pd, w_t):
    bsz, n, _ = pd.shape
    tr = _tile(n, 256)
    tc = DN_HEADS * DN_DK
    specs = _conv_specs(n, tr, tc, lambda c: c)
    return pl.pallas_call(
        _gdn_prep_kernel, grid=(bsz, n // tr, 3),
        in_specs=specs + [pl.BlockSpec((8, tc), lambda b, t, c: (0, c))],
        out_specs=pl.BlockSpec((1, tr, tc), lambda b, t, c: (b, t, c)),
        out_shape=jax.ShapeDtypeStruct((bsz, n, 3 * tc), F32),
        name="gdn_prep", compiler_params=_params(("parallel", "parallel", "parallel")),
    )(pd, pd, pd, w_t)


def _gdn_gate_kernel(x_ref, alog_ref, dtb_ref, o_ref):
    x = x_ref[0]
    lane = lax.broadcasted_iota(jnp.int32, x.shape, 1)
    z = x + dtb_ref[...]
    softplus = jnp.maximum(z, 0.0) + jnp.log(1.0 + jnp.exp(-jnp.abs(z)))
    g = -jnp.exp(alog_ref[...]) * softplus
    o_ref[0] = jnp.where(lane < 2 * DN_HEADS, g, _sigmoid(x))


def _gdn_gates(pab, alog, dtb):
    bsz, n, w = pab.shape
    tr = _tile(n, 512)
    spec = pl.BlockSpec((1, tr, w), lambda b, t: (b, t, 0))
    row = pl.BlockSpec((1, w), lambda b, t: (0, 0))
    return pl.pallas_call(
        _gdn_gate_kernel, grid=(bsz, n // tr), in_specs=[spec, row, row], out_specs=spec,
        out_shape=jax.ShapeDtypeStruct((bsz, n, w), F32),
        name="gdn_gates", compiler_params=_params(("parallel", "parallel")),
    )(pab, alog, dtb)


def _gdn_chunk_kernel(q_ref, k_ref, v_ref, gb_ref, gbt_ref, wq_ref, u_ref, lq_ref, eg_ref):
    cs, nh = DN_CHUNK, DN_HEADS
    r = lax.broadcasted_iota(jnp.int32, (cs, cs), 0)
    c = lax.broadcasted_iota(jnp.int32, (cs, cs), 1)
    eye = jnp.where(r == c, 1.0, 0.0)
    r2 = lax.broadcasted_iota(jnp.int32, (DN_DK, DN_DK), 0)
    c2 = lax.broadcasted_iota(jnp.int32, (DN_DK, DN_DK), 1)
    eye_b = jnp.where(r2 == c2, 1.0, 0.0).astype(BF16)
    gb = gb_ref[0]
    gbt = gbt_ref[0, 0]
    q, k, v = q_ref[0], k_ref[0], v_ref[0]
    heads = []
    for h in range(nh):
        sl = slice(h * DN_DK, (h + 1) * DN_DK)
        kh, qh, vh = k[:, sl], q[:, sl], v[:, sl]
        kb = kh.astype(BF16)
        both = _dot_nt(jnp.concatenate([kb, qh.astype(BF16)], axis=0), kb)
        heads.append((kh, qh, vh, both[:cs], both[cs:]))
    items = []
    for d in range(2):
        incl = (r >= c) if d == 0 else (r <= c)
        strict = (r > c) if d == 0 else (r < c)
        gcum_c = _dot3(jnp.where(incl, 1.0, 0.0), gb)
        gcum_r = _dot3(gbt, jnp.where(incl, 0.0, 1.0) + eye)
        for h in range(nh):
            kh, qh, vh, kk, qkt = heads[h]
            col = d * nh + h
            gc = gcum_c[:, col:col + 1]
            gr = gcum_r[col:col + 1, :]
            bc = gb[:, 2 * nh + col:2 * nh + col + 1]
            glast = gc[cs - 1:cs, :] if d == 0 else gc[0:1, :]
            decay = jnp.where(incl, jnp.exp(jnp.where(incl, gc - gr, 0.0)), 0.0)
            egc = jnp.exp(gc)
            eg_ref[0, 0, d, h:h + 1, :] = jnp.broadcast_to(jnp.exp(glast), (1, DN_DV))
            items.append(dict(
                d=d, h=h, a=jnp.where(strict, bc * kk * decay, 0.0),
                rhs=jnp.concatenate([vh * bc, kh * (bc * egc)], axis=1).astype(BF16),
                qkd=(qkt * decay).astype(BF16), q_dec=(qh * egc).astype(BF16),
                k_dec=(kh * jnp.exp(glast - gc)).astype(BF16)))
    for it in items:
        it["p"] = eye - it["a"]
        it["sq"] = it["a"]
    for _ in range(int(math.log2(cs)) - 1):
        for it in items:
            sqb = it["sq"].astype(BF16)
            it["sq"] = _dot(sqb, sqb)
        for it in items:
            it["p"] = it["p"] + _dot(it["p"].astype(BF16), it["sq"].astype(BF16))
    for it in items:
        it["res"] = eye - it["p"] - _dot3(it["a"], it["p"])
    for it in items:
        it["t"] = (it["p"] + _dot(it["p"].astype(BF16), it["res"].astype(BF16))).astype(BF16)
    for it in items:
        uw = _dot(it["t"], it["rhs"])
        d, h = it["d"], it["h"]
        u_ref[0, 0, d, h] = uw[:, :DN_DV]
        wq_ref[0, 0, d, h, 0:cs, :] = uw[:, DN_DV:].astype(BF16)
        wq_ref[0, 0, d, h, cs:2 * cs, :] = it["q_dec"]
        lq_ref[0, 0, d, h, 0:cs, :] = it["qkd"]
        lq_ref[0, 0, d, h, cs:, :] = _dot_nt(eye_b, it["k_dec"]).astype(BF16)


def _gdn_chunks(qkv, gb, gbt):
    bsz, n, _ = qkv.shape
    cs = DN_CHUNK
    nc = n // cs
    w = DN_HEADS * DN_DK
    row = lambda col: pl.BlockSpec((1, cs, w), lambda b, s, col=col: (b, s, col))

    def out(shape):
        return pl.BlockSpec((1, 1) + shape, lambda b, s: (b, s) + (0,) * len(shape))

    lead = (bsz, nc, 2, DN_HEADS)
    return pl.pallas_call(
        _gdn_chunk_kernel, grid=(bsz, nc),
        in_specs=[row(0), row(1), row(2), pl.BlockSpec((1, cs, LANES), lambda b, s: (b, s, 0)),
                  pl.BlockSpec((1, 1, LANES, cs), lambda b, s: (b, s, 0, 0))],
        out_specs=[out((2, DN_HEADS, 2 * cs, DN_DK)), out((2, DN_HEADS, cs, DN_DV)),
                   out((2, DN_HEADS, cs + DN_DK, cs)), out((2, DN_HEADS, DN_DV))],
        out_shape=[jax.ShapeDtypeStruct(lead + (2 * cs, DN_DK), BF16),
                   jax.ShapeDtypeStruct(lead + (cs, DN_DV), F32),
                   jax.ShapeDtypeStruct(lead + (cs + DN_DK, cs), BF16),
                   jax.ShapeDtypeStruct(lead + (DN_DV,), F32)],
        name="gdn_chunks", compiler_params=_params(("parallel", "parallel")),
    )(qkv, qkv, qkv, gb, gbt)


def _gdn_rec_kernel(wqf_ref, uf_ref, lqf_ref, egf_ref, wqb_ref, ub_ref, lqb_ref, egb_ref, s0_ref,
                    of_ref, ob_ref, st_ref):
    cs, nh = DN_CHUNK, DN_HEADS

    @pl.when(pl.program_id(1) == 0)
    def _():
        st_ref[...] = s0_ref[...]

    chains = []
    for d, (wq_ref, u_ref, lq_ref, eg_ref) in enumerate(((wqf_ref, uf_ref, lqf_ref, egf_ref),
                                                          (wqb_ref, ub_ref, lqb_ref, egb_ref))):
        for h in range(nh):
            chains.append(dict(d=d, h=h, wq=wq_ref[0, 0, 0, h], u=u_ref[0, 0, 0, h], lq=lq_ref[0, 0, 0, h],
                               eg=eg_ref[0, 0, 0, h:h + 1, :], s=st_ref[0, d, h]))
    for ch in chains:
        ch["r1"] = _dot(ch["wq"], ch["s"].astype(BF16))
    for ch in chains:
        ch["vn"] = (ch["u"] - ch["r1"][:cs]).astype(BF16)
    for ch in chains:
        ch["r2"] = _dot(ch["lq"], ch["vn"])
    outs = [[None] * nh, [None] * nh]
    for ch in chains:
        outs[ch["d"]][ch["h"]] = ch["r1"][cs:] + ch["r2"][:cs]
        st_ref[0, ch["d"], ch["h"]] = ch["s"] * ch["eg"] + ch["r2"][cs:]
    of_ref[0] = jnp.concatenate(outs[0], axis=1)
    ob_ref[0] = jnp.concatenate(outs[1], axis=1)


def _gdn_recurrence(wq, u, lq, eg, s0):
    bsz, nc = wq.shape[:2]
    cs = DN_CHUNK
    w = DN_HEADS * DN_DK

    def spec(arr, d):
        shape = arr.shape[3:]
        zeros = (0,) * len(shape)
        if d == 0:
            return pl.BlockSpec((1, 1, 1) + shape, lambda b, s: (b, s, 0) + zeros)
        return pl.BlockSpec((1, 1, 1) + shape, lambda b, s: (b, nc - 1 - s, 1) + zeros)

    st_spec = pl.BlockSpec((1, 2, DN_HEADS, DN_DK, DN_DV), lambda b, s: (b, 0, 0, 0, 0))
    return pl.pallas_call(
        _gdn_rec_kernel, grid=(bsz, nc),
        in_specs=[spec(wq, 0), spec(u, 0), spec(lq, 0), spec(eg, 0),
                  spec(wq, 1), spec(u, 1), spec(lq, 1), spec(eg, 1), st_spec],
        out_specs=[pl.BlockSpec((1, cs, w), lambda b, s: (b, s, 0)),
                   pl.BlockSpec((1, cs, w), lambda b, s: (b, nc - 1 - s, 0)),
                   st_spec],
        out_shape=[jax.ShapeDtypeStruct((bsz, nc * cs, w), F32), jax.ShapeDtypeStruct((bsz, nc * cs, w), F32),
                   jax.ShapeDtypeStruct((bsz, 2, DN_HEADS, DN_DK, DN_DV), F32)],
        name="gdn_recurrence", compiler_params=_params(("parallel", "arbitrary")),
    )(wq, u, lq, eg, wq, u, lq, eg, s0)


def _gdn_post_kernel(of_ref, ob_ref, z_ref, g_ref, y_ref):
    o = of_ref[0] + ob_ref[0]
    z = z_ref[0]
    outs = []
    for h in range(DN_HEADS):
        sl = slice(h * DN_DV, (h + 1) * DN_DV)
        t = o[:, sl]
        outs.append(t * lax.rsqrt(jnp.mean(t * t, axis=-1, keepdims=True) + EPS) * g_ref[...])
    y_ref[0] = (jnp.concatenate(outs, axis=1) * _silu(z)).astype(y_ref.dtype)


def _gdn_post(o_pair, pd, g_norm):
    o_f, o_b = o_pair
    bsz, n, w = o_f.shape
    tr = _tile(n, 256)
    spec = pl.BlockSpec((1, tr, w), lambda b, t: (b, t, 0))
    return pl.pallas_call(
        _gdn_post_kernel, grid=(bsz, n // tr),
        in_specs=[spec, spec, pl.BlockSpec((1, tr, w), lambda b, t: (b, t, 3)),
                  pl.BlockSpec((1, DN_DV), lambda b, t: (0, 0))],
        out_specs=spec, out_shape=jax.ShapeDtypeStruct((bsz, n, w), BF16),
        name="gdn_post", compiler_params=_params(("parallel", "parallel")),
    )(o_f, o_b, pd, g_norm)


def _gdn_segment(pd, pab, conv_t, alog, dtb, s0):
    bsz, n, _ = pd.shape
    qkv = _gdn_prep(pd, conv_t)
    gb = _gdn_gates(pab, alog, dtb)
    gbt = jnp.swapaxes(gb.reshape(bsz, n // DN_CHUNK, DN_CHUNK, LANES), 2, 3)
    o_f, o_b, s_fin = _gdn_recurrence(*_gdn_chunks(qkv, gb, gbt), s0)
    return (o_f, o_b), s_fin


def _hy_prep_kernel(*refs):
    w_ref, u_ref, x0_ref = refs[9], refs[10], refs[11]
    w = w_ref[...]
    v = _short_conv(refs[0], refs[1], refs[2], w[0], HY_CONV)
    x1 = _short_conv(refs[3], refs[4], refs[5], w[1], HY_CONV)
    x0 = _short_conv(refs[6], refs[7], refs[8], w[2], HY_CONV)
    u_ref[0] = x1 * v
    x0_ref[0] = x0


def _hy_prep(ph, w3):
    bsz, n, _ = ph.shape
    tr = _tile(n, 256)
    tc = 512
    ncb = HY_WIDTH // tc
    specs = []
    for part in range(3):
        specs += _conv_specs(n, tr, tc, lambda c, part=part: part * ncb + c)
    out_spec = pl.BlockSpec((1, tr, tc), lambda b, t, c: (b, t, c))
    return pl.pallas_call(
        _hy_prep_kernel, grid=(bsz, n // tr, ncb), name="hy_prep",
        in_specs=specs + [pl.BlockSpec((3, 8, tc), lambda b, t, c: (0, 0, c))],
        out_specs=[out_spec, out_spec],
        out_shape=[jax.ShapeDtypeStruct((bsz, n, HY_WIDTH), F32)] * 2,
        compiler_params=_params(("parallel", "parallel", "parallel")),
    )(*([ph] * 9), w3)


DFT_BLK = 256


def _dft_tables(n):
    big = 2 * n
    k = jnp.arange(n, dtype=jnp.int32)[:, None]
    t = jnp.arange(n, dtype=jnp.int32)[None, :]
    ang = ((k * t) % big).astype(F32) * (2.0 * math.pi / big)
    f_re = jnp.cos(ang)
    f_im = jnp.where(k == 0, (1 - 2 * (t % 2)).astype(F32), -jnp.sin(ang))
    blocks = jnp.stack([f_re.reshape(n // DFT_BLK, DFT_BLK, n), f_im.reshape(n // DFT_BLK, DFT_BLK, n)], axis=1)
    return blocks.reshape(2 * n, n)


def _colabs_kernel(c_ref, o_ref):
    o_ref[...] = jnp.sum(jnp.sum(jnp.abs(c_ref[...]), axis=0), axis=0, keepdims=True)


def _coef_kernel(fh_ref, nrm_ref, o_ref, *, n):
    kidx = pl.program_id(0) * DFT_BLK + lax.broadcasted_iota(jnp.int32, fh_ref.shape[3:], 0)
    sgn = (1 - 2 * (kidx % 2)).astype(F32)
    first = kidx == 0
    fh = fh_ref[:, 0]
    p = fh[0, 0] + sgn * fh[1, 0]
    qraw = fh[0, 1] + jnp.where(first, 1.0, sgn) * fh[1, 1]
    dk = jnp.where(first, 1.0 / (2 * n), 2.0 / (2 * n)) * (1.0 / nrm_ref[...])
    q_m = jnp.where(first, 0.0, qraw)
    o_ref[0] = dk * p
    o_ref[1] = -dk * q_m
    o_ref[2] = dk * q_m
    o_ref[3] = dk * jnp.where(first, qraw, p)


def _hy_filter_coefs(n, f_tab, w1p, b1, w2p, b2, f0, f1, w3, feat, tcol, rates2):
    h = _mm(feat[None], w1p[None], x3=True, extras=[(b1, "row"), (f0, "row")],
            epilogue=lambda acc, b, f: jnp.sin(f * (acc + b)))
    h = _mm(h, w2p[None], x3=True, extras=[(b2, "row"), (f1, "row")],
            epilogue=lambda acc, b, f: jnp.sin(f * (acc + b)))
    h = _mm(h, w3[None], x3=True, extras=[(tcol, "col"), (rates2, "row")],
            epilogue=lambda acc, t, rt: acc * jnp.exp(-t * rt))[0]
    c_lo = h[:, :HY_WIDTH]
    hb = h[:, HY_WIDTH:]
    c_hi = jnp.concatenate([jnp.zeros((1, HY_WIDTH), F32), hb[:0:-1]], axis=0)
    c2 = jnp.stack([c_lo, c_hi])
    tc = 256
    nrm = pl.pallas_call(
        _colabs_kernel, grid=(HY_WIDTH // tc,),
        in_specs=[pl.BlockSpec((2, n, tc), lambda j: (0, 0, j))],
        out_specs=pl.BlockSpec((1, tc), lambda j: (0, j)),
        out_shape=jax.ShapeDtypeStruct((1, HY_WIDTH), F32),
        name="hy_filter_l1", compiler_params=_params(("parallel",)),
    )(c2)
    nb = n // DFT_BLK
    fh = _mm(f_tab[None], c2, x3=True, name="hy_filter_dft").reshape(2, nb, 2, DFT_BLK, HY_WIDTH)
    return pl.pallas_call(
        functools.partial(_coef_kernel, n=n), grid=(nb, HY_WIDTH // tc),
        in_specs=[pl.BlockSpec((2, 1, 2, DFT_BLK, tc), lambda t, j: (0, t, 0, 0, j)),
                  pl.BlockSpec((1, tc), lambda t, j: (0, j))],
        out_specs=pl.BlockSpec((4, DFT_BLK, tc), lambda t, j: (0, t, j)),
        out_shape=jax.ShapeDtypeStruct((4, n, HY_WIDTH), F32),
        name="hy_filter_coef", compiler_params=_params(("parallel", "parallel")),
    )(fh, nrm)


def _freq_mul(acc, cf):
    a, b = acc[:DFT_BLK], acc[DFT_BLK:]
    return jnp.concatenate([a * cf[0] + b * cf[1], a * cf[2] + b * cf[3]], axis=0)


def _hyena(ph, conv3, coef, f_tab, ft_tab, d_skip):
    bsz, n, _ = ph.shape
    u, x0 = _hy_prep(ph, conv3)
    coef_spec = lambda tm, tn: pl.BlockSpec((1, 4, DFT_BLK, tn), lambda gi, j, i, kk: (0, 0, i, j))
    y = _mm(f_tab[None], u, out_dtype=BF16, tm=2 * DFT_BLK, extras=[(coef[None], coef_spec)],
            epilogue=_freq_mul, name="hy_dft_filter")
    return _mm(ft_tab[None], y, out_dtype=BF16,
               extras=[(u, "tile"), (x0, "tile"), (d_skip, "row")],
               epilogue=lambda acc, uu, xx, dd: xx * (acc + dd * uu), name="hy_idft_gate")


def _merge_kernel(ya_ref, yd_ref, yh_ref, g0_ref, g1_ref, g2_ref, w_ref, o_ref):
    acc = _sigmoid(g0_ref[...]) * _dot(ya_ref[...], w_ref[0])
    acc = acc + _sigmoid(g1_ref[...]) * _dot(yd_ref[...], w_ref[1])
    acc = acc + _sigmoid(g2_ref[...]) * _dot(yh_ref[...], w_ref[2])
    o_ref[...] = acc.astype(o_ref.dtype)


def _merge(ya, yd, yh, pg, w_branch):
    m = ya.shape[0]
    tm = _tile(m, 512)
    tn = 512
    nj = D_MODEL // tn
    br = pl.BlockSpec((tm, ya.shape[1]), lambda j, i: (i, 0))
    gate = lambda r: pl.BlockSpec((tm, tn), lambda j, i, r=r: (i, r * nj + j))
    return pl.pallas_call(
        _merge_kernel, grid=(nj, m // tm), name="branch_merge",
        in_specs=[br, br, br, gate(0), gate(1), gate(2),
                  pl.BlockSpec((N_BRANCH, ya.shape[1], tn), lambda j, i: (0, 0, j))],
        out_specs=pl.BlockSpec((tm, tn), lambda j, i: (i, j)),
        out_shape=jax.ShapeDtypeStruct((m, D_MODEL), BF16),
        compiler_params=_params(("parallel", "parallel")),
    )(ya, yd, yh, pg, pg, pg, w_branch)


def _select_kernel(aff_ref, tri_ref, slot_ref, *, cap):
    aff = aff_ref[0]
    bits = pltpu.bitcast(aff, jnp.int32)
    thr = jnp.zeros((aff.shape[0], 1), jnp.int32)
    for bit in range(30, -1, -1):
        cand = thr | (1 << bit)
        cnt = jnp.sum(jnp.where(bits >= cand, 1.0, 0.0), axis=1, keepdims=True)
        thr = jnp.where(cnt >= cap, cand, thr)
    gt = bits > thr
    eq = bits == thr
    need = cap - jnp.sum(jnp.where(gt, 1.0, 0.0), axis=1, keepdims=True)
    tri = tri_ref[...]
    eq_before = _dot(jnp.where(eq, 1.0, 0.0).astype(BF16), tri)
    sel = gt | (eq & (eq_before < need))
    slot = _dot(jnp.where(sel, 1.0, 0.0).astype(BF16), tri)
    slot_ref[0] = jnp.where(sel, slot, -1.0).astype(jnp.int32)


def _select(aff_t, tri, cap):
    bsz, ne, n = aff_t.shape
    return pl.pallas_call(
        functools.partial(_select_kernel, cap=cap), grid=(bsz,), name="ffn_select",
        in_specs=[pl.BlockSpec((1, ne, n), lambda b: (b, 0, 0)), pl.BlockSpec((n, n), lambda b: (0, 0))],
        out_specs=pl.BlockSpec((1, ne, n), lambda b: (b, 0, 0)),
        out_shape=jax.ShapeDtypeStruct((bsz, ne, n), jnp.int32),
        compiler_params=_params(("parallel",)),
    )(aff_t, tri)


def _gather_kernel(slot_ref, aff_ref, h_ref, xe_ref, gs_ref, *, cap):
    e = pl.program_id(1)
    slot = slot_ref[0, pl.ds(e, 1), :]
    aff = aff_ref[0, pl.ds(e, 1), :]
    n = slot.shape[1]
    hit = lax.broadcasted_iota(jnp.int32, (cap, n), 0) == slot
    xe_ref[0, 0] = _dot(jnp.where(hit, 1.0, 0.0).astype(BF16), h_ref[0]).astype(xe_ref.dtype)
    gs_ref[0, 0] = jnp.sum(jnp.where(hit, aff, 0.0), axis=1, keepdims=True)


def _gather(slot, aff_t, h, cap):
    bsz, ne, n = slot.shape
    d = h.shape[2]
    row = pl.BlockSpec((1, ne, n), lambda b, e: (b, 0, 0))
    return pl.pallas_call(
        functools.partial(_gather_kernel, cap=cap), grid=(bsz, ne), name="ffn_gather",
        in_specs=[row, row, pl.BlockSpec((1, n, d), lambda b, e: (b, 0, 0))],
        out_specs=[pl.BlockSpec((1, 1, cap, d), lambda b, e: (e, b, 0, 0)),
                   pl.BlockSpec((1, 1, cap, 1), lambda b, e: (e, b, 0, 0))],
        out_shape=[jax.ShapeDtypeStruct((ne, bsz, cap, d), BF16),
                   jax.ShapeDtypeStruct((ne, bsz, cap, 1), F32)],
        compiler_params=_params(("parallel", "arbitrary")),
    )(slot, aff_t, h)


def _expert_kernel(x_ref, gs_ref, wg_ref, wu_ref, wd_ref, y_ref):
    x = x_ref[0]
    a = _dot(x, wg_ref[0])
    u = _dot(x, wu_ref[0])
    y = _dot((_silu(a) * u).astype(BF16), wd_ref[0])
    y_ref[0] = (y * gs_ref[0]).astype(y_ref.dtype)


def _experts(xe, gs, wg, wu, wd):
    ne, m, d = xe.shape
    ff = wg.shape[2]
    tm = _tile(m, 512)
    return pl.pallas_call(
        _expert_kernel, grid=(ne, m // tm), name="ffn_experts",
        in_specs=[pl.BlockSpec((1, tm, d), lambda e, i: (e, i, 0)),
                  pl.BlockSpec((1, tm, 1), lambda e, i: (e, i, 0)),
                  pl.BlockSpec((1, d, ff), lambda e, i: (e, 0, 0)),
                  pl.BlockSpec((1, d, ff), lambda e, i: (e, 0, 0)),
                  pl.BlockSpec((1, ff, d), lambda e, i: (e, 0, 0))],
        out_specs=pl.BlockSpec((1, tm, d), lambda e, i: (e, i, 0)),
        out_shape=jax.ShapeDtypeStruct((ne, m, d), BF16),
        compiler_params=_params(("parallel", "parallel")),
    )(xe, gs, wg, wu, wd)


def _scatter_kernel(slot_ref, ye_ref, x_ref, g_ref, o_ref, *, cap):
    ne = slot_ref.shape[1]
    tr = slot_ref.shape[2]
    lane = lax.broadcasted_iota(jnp.int32, (tr, cap), 1)
    hit = jnp.concatenate([jnp.where(lane == slot_ref[0, e], 1.0, 0.0).astype(BF16) for e in range(ne)],
                          axis=1)
    ye = ye_ref[...].reshape(ne * cap, ye_ref.shape[3])
    o_ref[0] = x_ref[0] + g_ref[0] * _dot(hit, ye)


def _scatter_residual(slot_col, ye, x, gate_row, cap):
    bsz, ne, n, _ = slot_col.shape
    d = x.shape[2]
    tr = _tile(n, 256)
    xs = pl.BlockSpec((1, tr, d), lambda b, i: (b, i, 0))
    return pl.pallas_call(
        functools.partial(_scatter_kernel, cap=cap), grid=(bsz, n // tr),
        in_specs=[pl.BlockSpec((1, ne, tr, 1), lambda b, i: (b, 0, i, 0)),
                  pl.BlockSpec((ne, 1, cap, d), lambda b, i: (0, b, 0, 0)),
                  xs, pl.BlockSpec((1, 1, d), lambda b, i: (b, 0, 0))],
        out_specs=xs, out_shape=jax.ShapeDtypeStruct((bsz, n, d), F32),
        name="ffn_scatter", compiler_params=_params(("parallel", "arbitrary")),
    )(slot_col, ye, x, gate_row)


def _ffn(x, gain, mod, router_t, wg, wu, wd, tri):
    bsz, n, d = x.shape
    cap = EC_CAPACITY * n // N_EXPERTS
    h, aff_t = _normmod(x, gain, mod, 3, 4, router_t)
    slot = _select(aff_t, tri, cap)
    xe, gs = _gather(slot, aff_t, h, cap)
    ye = _experts(xe.reshape(N_EXPERTS, bsz * cap, d), gs.reshape(N_EXPERTS, bsz * cap, 1), wg, wu, wd)
    return _scatter_residual(slot[..., None], ye.reshape(N_EXPERTS, bsz, cap, d), x, mod[:, 5:6], cap)


def _hy_consts(n):
    t = jnp.linspace(0.0, 1.0, n, dtype=F32)[:, None]
    pos = jnp.arange(n, dtype=F32)
    bands = jnp.linspace(1e-4, HY_BANDS - 1, HY_BANDS, dtype=F32)
    ang = (2.0 * math.pi / n) * pos[:, None] * bands[None, :]
    feat = jnp.concatenate([t, jnp.cos(ang), -jnp.sin(ang)], axis=-1)
    feat = jnp.pad(feat, ((0, 0), (0, LANES - HY_EMB)))
    f_tab = _dft_tables(n)
    return dict(feat=feat, tcol=t[None], f_tab=f_tab, f_bf=f_tab.astype(BF16),
                ft_bf=jnp.transpose(f_tab).astype(BF16))


def _mixer_segment_proj(h, weights):
    bsz, n, d = h.shape
    hf = h.reshape(1, bsz * n, d)
    return [None if w is None else
            _mm(hf, w[None], tm=1024, tn=1024, name="in_proj").reshape(bsz, n, w.shape[1]) for w in weights]


def kernel(x, c, ctx, c_ctx, w_ada, b_ada, norm_mix, norm_ffn, w_in, diff_lambda, diff_norm, dn_conv,
           dn_a_log, dn_dt_bias, dn_norm, hy_conv, hy_w1, hy_b1, hy_w2, hy_b2, hy_freq, hy_w3, hy_bias,
           w_branch, w_out, router, w_gate, w_up, w_down, norm_final):
    depth = w_ada.shape[0]
    bsz, n_lat, d = x.shape
    n_ctx = ctx.shape[1]
    assert d == D_MODEL and n_lat % DFT_BLK == 0 and n_ctx % DFT_BLK == 0 and DFT_BLK % DN_CHUNK == 0

    rope_tabs = _rope_tables(n_lat)
    hy_l = _hy_consts(n_lat)
    hy_c = _hy_consts(n_ctx)
    tri_l = (jnp.arange(n_lat)[:, None] < jnp.arange(n_lat)[None, :]).astype(BF16)
    tri_c = (jnp.arange(n_ctx)[:, None] < jnp.arange(n_ctx)[None, :]).astype(BF16)
    rates = jnp.abs(jnp.linspace(HY_DECAY_MIN, HY_DECAY_MAX, HY_WIDTH, dtype=F32))
    rates2 = jnp.concatenate([rates, rates])[None, None]
    cc = jnp.concatenate([c, jnp.broadcast_to(c_ctx[None], (8, d))], axis=0)[None]

    o_aq, o_dq, o_da, o_hy, o_gt = 0, 3072, 7168, 7200, 10272

    for l in range(depth):
        with_ctx = l < depth - 1
        lam0 = jnp.full((1, 1), 0.8 - 0.6 * math.exp(-0.3 * l), F32)
        mod = _mm(cc, w_ada[l][None], tn=1024, prologue=_silu, extras=[(b_ada[l][None, None], "row")],
                  epilogue=lambda acc, bias: acc + bias, name="ada_ln")[0]
        mod_l = jnp.pad(mod[:bsz].reshape(bsz, 6, d), ((0, 0), (0, 2), (0, 0)))
        mod_c = jnp.broadcast_to(jnp.pad(mod[bsz].reshape(1, 6, d), ((0, 0), (0, 2), (0, 0))), (bsz, 8, d))

        wl = w_in[l]
        w_attn = wl[:, o_aq:o_dq].astype(BF16)
        w_dn = wl[:, o_dq:o_da].astype(BF16)
        w_ab = jnp.pad(wl[:, o_da:o_hy], ((0, 0), (0, LANES - 4 * DN_HEADS))).astype(BF16)
        w_hy = wl[:, o_hy:o_gt].astype(BF16)
        w_gt = wl[:, o_gt:].astype(BF16)
        wb = w_branch[l].astype(BF16)
        wo = w_out[l].astype(BF16)
        wg, wu, wd = w_gate[l].astype(BF16), w_up[l].astype(BF16), w_down[l].astype(BF16)
        router_t = jnp.transpose(router[l])
        gain_mix, gain_ffn = norm_mix[l][None], norm_ffn[l][None]
        conv_dn = jnp.pad(jnp.transpose(dn_conv[l]), ((0, 8 - DN_CONV), (0, 0)))
        conv_hy = jnp.pad(jnp.transpose(hy_conv[l]), ((0, 8 - HY_CONV), (0, 0)))
        conv_hy = jnp.transpose(conv_hy.reshape(8, 3, HY_WIDTH), (1, 0, 2))
        alog = jnp.pad(dn_a_log[l].reshape(1, -1), ((0, 0), (0, LANES - 2 * DN_HEADS)))
        dtb = jnp.pad(dn_dt_bias[l].reshape(1, -1), ((0, 0), (0, LANES - 2 * DN_HEADS)))
        pad_h = lambda w: jnp.pad(w, ((0, LANES - w.shape[0]), (0, LANES - w.shape[1])))
        pad_r = lambda v: jnp.pad(v, (0, LANES - v.shape[0]))[None, None]
        w1p = pad_h(hy_w1[l])
        w2p = pad_h(hy_w2[l])
        w3p = jnp.pad(hy_w3[l], ((0, LANES - HY_HIDDEN), (0, 0)))
        filt = lambda n, hc: _hy_filter_coefs(
            n, hc["f_tab"], w1p, pad_r(hy_b1[l]), w2p, pad_r(hy_b2[l]), pad_r(hy_freq[l, 0]),
            pad_r(hy_freq[l, 1]), w3p, hc["feat"], hc["tcol"], rates2)
        d_skip = hy_bias[l][None, None]
        g_dn = dn_norm[l][None]
        g_diff = diff_norm[l][None]

        h_l = _normmod(x, gain_mix, mod_l, 0, 1)
        h_c = _normmod(ctx, gain_mix, mod_c, 0, 1)
        pa_l, pd_l, ph_l, pg_l, pab_l = _mixer_segment_proj(h_l, (w_attn, w_dn, w_hy, w_gt, w_ab))
        pa_c, pd_c, ph_c, pg_c, pab_c = _mixer_segment_proj(
            h_c, (w_attn, w_dn, w_hy if with_ctx else None, w_gt if with_ctx else None, w_ab))

        ya_l = _attention(pa_l, pa_c, diff_lambda[l], g_diff, lam0, rope_tabs)
        s0 = jnp.zeros((bsz, 2, DN_HEADS, DN_DK, DN_DV), F32)
        o_c, s_c = _gdn_segment(pd_c, pab_c, conv_dn, alog, dtb, s0)
        o_l, _ = _gdn_segment(pd_l, pab_l, conv_dn, alog, dtb, s_c)
        yd_l = _gdn_post(o_l, pd_l, g_dn)
        yh_l = _hyena(ph_l, conv_hy, filt(n_lat, hy_l), hy_l["f_bf"], hy_l["ft_bf"], d_skip)

        def merge_out(ya, yd, yh, pg, xs, mod_s):
            n = xs.shape[1]
            flat = lambda t: t.reshape(bsz * n, t.shape[2])
            y = _merge(flat(ya), flat(yd), flat(yh), flat(pg), wb).reshape(bsz, n, d)
            return _mm(y, wo[None], tm=1024, tn=1024, extras=[(xs, "tile"), (mod_s[:, 2:3], "row")],
                       epilogue=lambda acc, xx, gg: xx + gg * acc, name="out_proj_residual")

        x = merge_out(ya_l, yd_l, yh_l, pg_l, x, mod_l)
        x = _ffn(x, gain_ffn, mod_l, router_t, wg, wu, wd, tri_l)
        if with_ctx:
            ya_c = _attention(pa_c, None, diff_lambda[l], g_diff, lam0, None)
            yd_c = _gdn_post(o_c, pd_c, g_dn)
            yh_c = _hyena(ph_c, conv_hy, filt(n_ctx, hy_c), hy_c["f_bf"], hy_c["ft_bf"], d_skip)
            ctx = merge_out(ya_c, yd_c, yh_c, pg_c, ctx, mod_c)
            ctx = _ffn(ctx, gain_ffn, mod_c, router_t, wg, wu, wd, tri_c)
    return _final_norm(x, norm_final[None])
```

```python
import functools
import math

import jax
import jax.numpy as jnp
from jax import lax
from jax.experimental import pallas as pl
from jax.experimental.pallas import tpu as pltpu

F32 = jnp.float32
BF16 = jnp.bfloat16

D_MODEL = 2048
GRID_W = 64
EPS = 1e-6
A_HEADS = 8
A_DIM = 64
A_VDIM = 128
ROPE_BASE = 10000.0
DN_HEADS = 8
DN_DK = 128
DN_DV = 128
DN_CONV = 5
DN_CHUNK = 64
HY_WIDTH = 1024
HY_CONV = 3
HY_BANDS = 16
HY_EMB = 1 + 2 * HY_BANDS
HY_HIDDEN = 64
HY_DECAY_MIN = math.log(1e-2) / 1.5
HY_DECAY_MAX = math.log(1e-2) / 0.3
N_BRANCH = 3
N_EXPERTS = 16
EXPERT_FF = 1024
EC_CAPACITY = 2

LANES = 128
SUBLANES = 8
HALO = SUBLANES
VMEM_LIMIT = 56 * 1024 * 1024


def _params(sem, vmem=VMEM_LIMIT):
    return pltpu.CompilerParams(dimension_semantics=sem, vmem_limit_bytes=vmem)


def _tile(n, pref):
    if n <= pref:
        return n
    t = pref
    while n % t:
        t -= SUBLANES
    return t


def _split3(x):
    hi = x.astype(BF16)
    lo = (x - hi.astype(F32)).astype(BF16)
    return hi, lo


def _dot(a, b):
    return jnp.dot(a, b, preferred_element_type=F32)


def _dot3(a, b):
    ah, al = _split3(a)
    bh, bl = _split3(b)
    return _dot(ah, bh) + (_dot(ah, bl) + _dot(al, bh))


def _dot_nt(a, b):
    return lax.dot_general(a, b, (((1,), (1,)), ((), ())), preferred_element_type=F32)


def _dot3_nt(a, b):
    ah, al = _split3(a)
    bh, bl = _split3(b)
    return _dot_nt(ah, bh) + (_dot_nt(ah, bl) + _dot_nt(al, bh))


def _sigmoid(x):
    return 0.5 * (1.0 + jnp.tanh(0.5 * x))


def _silu(x):
    return x * _sigmoid(x)


def _mm_kernel(*refs, nk, n_extra, epilogue, prologue, x3):
    a_ref, b_ref = refs[0], refs[1]
    extra_refs = refs[2:2 + n_extra]
    o_ref = refs[2 + n_extra]
    a = a_ref[0]
    b = b_ref[0]
    if prologue is not None:
        a = prologue(a)
    if x3:
        part = _dot3(a.astype(F32), b.astype(F32))
    else:
        part = _dot(a.astype(BF16), b.astype(BF16))

    def finish(acc):
        if epilogue is not None:
            acc = epilogue(acc, *[r[0] for r in extra_refs])
        o_ref[0] = acc.astype(o_ref.dtype)

    if nk == 1:
        finish(part)
    else:
        acc_ref = refs[3 + n_extra]
        k = pl.program_id(3)

        @pl.when(k == 0)
        def _():
            acc_ref[...] = part

        @pl.when(k > 0)
        def _():
            acc_ref[...] += part

        @pl.when(k == nk - 1)
        def _():
            finish(acc_ref[...])


def _mm(a, b, *, out_dtype=F32, tm=512, tn=512, tk=None, extras=(), epilogue=None, prologue=None,
        x3=False, name="matmul"):
    ga, m, k = a.shape
    gb, kb, n = b.shape
    assert k == kb
    g = max(ga, gb)
    tm = _tile(m, tm)
    tn = n if n <= tn else tn
    tk = k if tk is None else min(tk, k)
    assert m % tm == 0 and n % tn == 0 and k % tk == 0, (m, n, k, tm, tn, tk)
    ni, nj, nk = m // tm, n // tn, k // tk
    in_specs = [
        pl.BlockSpec((1, tm, tk), lambda gi, j, i, kk: (gi if ga > 1 else 0, i, kk)),
        pl.BlockSpec((1, tk, tn), lambda gi, j, i, kk: (gi if gb > 1 else 0, kk, j)),
    ]
    args = [a, b]
    for arr, mode in extras:
        ge = arr.shape[0]
        if callable(mode):
            spec = mode(tm, tn)
        elif mode == "tile":
            spec = pl.BlockSpec((1, tm, tn), lambda gi, j, i, kk, ge=ge: (gi if ge > 1 else 0, i, j))
        elif mode == "row":
            spec = pl.BlockSpec((1, 1, tn), lambda gi, j, i, kk, ge=ge: (gi if ge > 1 else 0, 0, j))
        else:
            spec = pl.BlockSpec((1, tm, 1), lambda gi, j, i, kk, ge=ge: (gi if ge > 1 else 0, i, 0))
        in_specs.append(spec)
        args.append(arr)
    scratch = [pltpu.VMEM((tm, tn), F32)] if nk > 1 else []
    return pl.pallas_call(
        functools.partial(_mm_kernel, nk=nk, n_extra=len(extras), epilogue=epilogue,
                          prologue=prologue, x3=x3),
        grid=(g, nj, ni, nk),
        in_specs=in_specs,
        out_specs=pl.BlockSpec((1, tm, tn), lambda gi, j, i, kk: (gi, i, j)),
        out_shape=jax.ShapeDtypeStruct((g, m, n), out_dtype),
        scratch_shapes=scratch, name=name,
        compiler_params=_params(("parallel", "parallel", "parallel", "arbitrary")),
    )(*args)


def _normmod_body(x_ref, g_ref, mod_ref, shift_idx, scale_idx):
    x = x_ref[0]
    y = x * lax.rsqrt(jnp.mean(x * x, axis=-1, keepdims=True) + EPS) * g_ref[...]
    m = mod_ref[0]
    return y * (1.0 + m[scale_idx:scale_idx + 1]) + m[shift_idx:shift_idx + 1]


def _normmod_kernel(x_ref, g_ref, mod_ref, h_ref, *, shift_idx, scale_idx):
    h_ref[0] = _normmod_body(x_ref, g_ref, mod_ref, shift_idx, scale_idx).astype(h_ref.dtype)


def _normmod_router_kernel(x_ref, g_ref, mod_ref, rt_ref, h_ref, aff_ref, *, shift_idx, scale_idx):
    h = _normmod_body(x_ref, g_ref, mod_ref, shift_idx, scale_idx)
    h_ref[0] = h.astype(h_ref.dtype)
    logits = _dot3_nt(rt_ref[...], h)
    e = jnp.exp(logits - jnp.max(logits, axis=0, keepdims=True))
    aff_ref[0] = e * (1.0 / jnp.sum(e, axis=0, keepdims=True))


def _normmod(x, gain, mod, shift_idx, scale_idx, router_t=None):
    bsz, n, d = x.shape
    tr = _tile(n, 256)
    x_spec = pl.BlockSpec((1, tr, d), lambda b, t: (b, t, 0))
    g_spec = pl.BlockSpec((1, d), lambda b, t: (0, 0))
    m_spec = pl.BlockSpec((1, 8, d), lambda b, t: (b, 0, 0))
    if router_t is None:
        return pl.pallas_call(
            functools.partial(_normmod_kernel, shift_idx=shift_idx, scale_idx=scale_idx),
            grid=(bsz, n // tr), in_specs=[x_spec, g_spec, m_spec], out_specs=x_spec,
            out_shape=jax.ShapeDtypeStruct((bsz, n, d), BF16),
            name="normmod", compiler_params=_params(("parallel", "parallel")),
        )(x, gain, mod)
    ne = router_t.shape[0]
    return pl.pallas_call(
        functools.partial(_normmod_router_kernel, shift_idx=shift_idx, scale_idx=scale_idx),
        grid=(bsz, n // tr),
        in_specs=[x_spec, g_spec, m_spec, pl.BlockSpec((ne, d), lambda b, t: (0, 0))],
        out_specs=[x_spec, pl.BlockSpec((1, ne, tr), lambda b, t: (b, 0, t))],
        out_shape=[jax.ShapeDtypeStruct((bsz, n, d), BF16), jax.ShapeDtypeStruct((bsz, ne, n), F32)],
        name="normmod_router", compiler_params=_params(("parallel", "parallel")),
    )(x, gain, mod, router_t)


def _final_norm_kernel(x_ref, g_ref, o_ref):
    x = x_ref[0]
    o_ref[0] = x * lax.rsqrt(jnp.mean(x * x, axis=-1, keepdims=True) + EPS) * g_ref[...]


def _final_norm(x, gain):
    bsz, n, d = x.shape
    tr = _tile(n, 256)
    spec = pl.BlockSpec((1, tr, d), lambda b, t: (b, t, 0))
    return pl.pallas_call(
        _final_norm_kernel, grid=(bsz, n // tr), name="final_norm",
        in_specs=[spec, pl.BlockSpec((1, d), lambda b, t: (0, 0))], out_specs=spec,
        out_shape=jax.ShapeDtypeStruct((bsz, n, d), F32),
        compiler_params=_params(("parallel", "parallel")),
    )(x, gain)


def _rope_tables(n):
    t = jnp.arange(n, dtype=jnp.int32)
    row = (t // GRID_W).astype(F32)
    col = (t % GRID_W).astype(F32)
    n_freq = A_DIM // 4
    inv = ROPE_BASE ** (-jnp.arange(n_freq, dtype=F32) / n_freq)
    ang = jnp.concatenate([row[:, None] * inv, col[:, None] * inv], axis=-1)
    cos, sin = jnp.cos(ang), jnp.sin(ang)
    return jnp.tile(cos, (1, 4)), jnp.concatenate([-sin, sin, -sin, sin], axis=-1)


def _rope(x, c, s):
    lane = lax.broadcasted_iota(jnp.int32, x.shape, 1)
    partner = jnp.where((lane % A_DIM) < A_DIM // 2,
                        pltpu.roll(x, LANES - A_DIM // 2, axis=1), pltpu.roll(x, A_DIM // 2, axis=1))
    return x * c + partner * s


ATTN_Q_TILE = 2048
ATTN_Q_SUB = 256


def _attn_kernel(*refs, rope, with_ctx):
    it = iter(refs)
    q_ref, k_ref, v_ref = next(it), next(it), next(it)
    if with_ctx:
        kc_ref, vc_ref = next(it), next(it)
    if rope:
        cq_ref, sq_ref, ck_ref, sk_ref = next(it), next(it), next(it), next(it)
    lam_ref, lam0_ref, g_ref, o_ref = next(it), next(it), next(it), next(it)
    kb_s, va_s = next(it), next(it)

    @pl.when(pl.program_id(2) == 0)
    def _():
        k = k_ref[0]
        n = k.shape[0]
        if rope:
            k = _rope(k, ck_ref[...], sk_ref[...])
        kb_s[0:n] = k.astype(BF16)
        va_s[0:n, :A_VDIM] = v_ref[0].astype(BF16)
        if with_ctx:
            kb_s[n:] = kc_ref[0].astype(BF16)
            va_s[n:, :A_VDIM] = vc_ref[0].astype(BF16)
        va_s[:, A_VDIM:] = jnp.ones((va_s.shape[0], A_VDIM), BF16)

    q = q_ref[0]
    if rope:
        q = _rope(q, cq_ref[...], sq_ref[...])
    q = q * (A_DIM ** -0.5 * math.log2(math.e))
    lane = lax.broadcasted_iota(jnp.int32, q.shape, 1)
    lp = lam_ref[...]
    lam0 = lam0_ref[...]
    lam = (jnp.exp(jnp.sum(lp[0:1] * lp[1:2], axis=1, keepdims=True))
           - jnp.exp(jnp.sum(lp[2:3] * lp[3:4], axis=1, keepdims=True)) + lam0)
    kb = kb_s[...]
    tq = q.shape[0]
    sub = ATTN_Q_SUB if tq % ATTN_Q_SUB == 0 else tq
    chains = [(r0, half) for r0 in range(0, tq, sub) for half in (0, 1)]
    qhs = [jnp.where((lane // A_DIM) == half, q, 0.0).astype(BF16) for half in (0, 1)]
    st = {}

    def scores(c):
        st[c] = _dot_nt(qhs[c[1]][c[0]:c[0] + sub], kb)

    def softmax_pv(c):
        s = st[c]
        e = jnp.exp2(s - jnp.max(s, axis=-1, keepdims=True)).astype(BF16)
        r = _dot(e, va_s[...])
        st[c] = r[:, :A_VDIM] * (1.0 / r[:, A_VDIM:A_VDIM + 1])

    scores(chains[0])
    for j, c in enumerate(chains):
        if j + 1 < len(chains):
            scores(chains[j + 1])
        softmax_pv(c)
    rows = [st[(r0, 0)] - lam * st[(r0, 1)] for r0 in range(0, tq, sub)]
    o = jnp.concatenate(rows, axis=0) if len(rows) > 1 else rows[0]
    y = o * lax.rsqrt(jnp.mean(o * o, axis=-1, keepdims=True) + EPS) * g_ref[...]
    o_ref[0] = (y * (1.0 - lam0)).astype(o_ref.dtype)


def _attention(pa, pa_ctx, lam_p, g_norm, lam0, tables):
    bsz, n, _ = pa.shape
    tq = _tile(n, ATTN_Q_TILE)
    h_off = A_HEADS
    in_specs = [
        pl.BlockSpec((1, tq, LANES), lambda b, h, i: (b, i, h)),
        pl.BlockSpec((1, n, LANES), lambda b, h, i: (b, 0, h_off + h)),
        pl.BlockSpec((1, n, LANES), lambda b, h, i: (b, 0, 2 * h_off + h)),
    ]
    args = [pa, pa, pa]
    with_ctx = pa_ctx is not None
    nc = pa_ctx.shape[1] if with_ctx else 0
    scratch = [pltpu.VMEM((n + nc, LANES), BF16), pltpu.VMEM((n + nc, 2 * A_VDIM), BF16)]
    if with_ctx:
        in_specs += [pl.BlockSpec((1, nc, LANES), lambda b, h, i: (b, 0, h_off + h)),
                     pl.BlockSpec((1, nc, LANES), lambda b, h, i: (b, 0, 2 * h_off + h))]
        args += [pa_ctx, pa_ctx]
    rope = tables is not None
    if rope:
        ct, st = tables
        in_specs += [pl.BlockSpec((tq, LANES), lambda b, h, i: (i, 0))] * 2
        in_specs += [pl.BlockSpec((n, LANES), lambda b, h, i: (0, 0))] * 2
        args += [ct, st, ct, st]
    in_specs += [pl.BlockSpec((4, A_DIM), lambda b, h, i: (0, 0)),
                 pl.BlockSpec((1, 1), lambda b, h, i: (0, 0)),
                 pl.BlockSpec((1, A_VDIM), lambda b, h, i: (0, 0))]
    args += [lam_p, lam0, g_norm]
    return pl.pallas_call(
        functools.partial(_attn_kernel, rope=rope, with_ctx=with_ctx),
        grid=(bsz, A_HEADS, n // tq), in_specs=in_specs,
        out_specs=pl.BlockSpec((1, tq, LANES), lambda b, h, i: (b, i, h)),
        out_shape=jax.ShapeDtypeStruct((bsz, n, A_HEADS * A_VDIM), BF16),
        scratch_shapes=scratch, name="diff_attention",
        compiler_params=_params(("parallel", "parallel", "arbitrary")),
    )(*args)


def _conv_specs(n, tr, tc, col_of):
    r = tr // HALO
    last = n // HALO - 1
    return [
        pl.BlockSpec((1, HALO, tc), lambda b, t, c: (b, jnp.maximum(t * r - 1, 0), col_of(c))),
        pl.BlockSpec((1, tr, tc), lambda b, t, c: (b, t, col_of(c))),
        pl.BlockSpec((1, HALO, tc), lambda b, t, c: (b, jnp.minimum((t + 1) * r, last), col_of(c))),
    ]


def _short_conv(prev_ref, x_ref, next_ref, w, taps):
    t = pl.program_id(1)
    nt = pl.num_programs(1)
    x = x_ref[0]
    tr = x.shape[0]
    prev = jnp.where(t > 0, prev_ref[0], 0.0)
    nxt = jnp.where(t < nt - 1, next_ref[0], 0.0)
    xe = jnp.concatenate([prev, x, nxt], axis=0)
    rows = tr + 2 * HALO
    pad = (taps - 1) // 2
    acc = None
    for j in range(taps):
        d = j - pad
        sh = xe if d == 0 else pltpu.roll(xe, (rows - d) % rows, axis=0)
        term = sh[HALO:HALO + tr] * w[j:j + 1]
        acc = term if acc is None else acc + term
    return acc


def _gdn_prep_kernel(prev_ref, x_ref, next_ref, w_ref, o_ref):
    y = _silu(_short_conv(prev_ref, x_ref, next_ref, w_ref[...], DN_CONV))
    c = pl.program_id(2)
    outs = []
    for h in range(DN_HEADS):
        t = y[:, h * DN_DK:(h + 1) * DN_DK]
        rs = lax.rsqrt(jnp.sum(t * t, axis=-1, keepdims=True) + EPS)
        f = jnp.where(c == 0, rs * (DN_DK ** -0.5), jnp.where(c == 1, rs, 1.0))
        outs.append(t * f)
    o_ref[0] = jnp.concatenate(outs, axis=1)


def _gdn_prep(pd, w_t):
    bsz, n, _ = pd.shape
    tr = _tile(n, 256)
    tc = DN_HEADS * DN_DK
    specs = _conv_specs(n, tr, tc, lambda c: c)
    return pl.pallas_call(
        _gdn_prep_kernel, grid=(bsz, n // tr, 3),
        in_specs=specs + [pl.BlockSpec((8, tc), lambda b, t, c: (0, c))],
        out_specs=pl.BlockSpec((1, tr, tc), lambda b, t, c: (b, t, c)),
        out_shape=jax.ShapeDtypeStruct((bsz, n, 3 * tc), F32),
        name="gdn_prep", compiler_params=_params(("parallel", "parallel", "parallel")),
    )(pd, pd, pd, w_t)


def _gdn_gate_kernel(x_ref, alog_ref, dtb_ref, o_ref):
    x = x_ref[0]
    lane = lax.broadcasted_iota(jnp.int32, x.shape, 1)
    z = x + dtb_ref[...]
    softplus = jnp.maximum(z, 0.0) + jnp.log(1.0 + jnp.exp(-jnp.abs(z)))
    g = -jnp.exp(alog_ref[...]) * softplus
    o_ref[0] = jnp.where(lane < 2 * DN_HEADS, g, _sigmoid(x))


def _gdn_gates(pab, alog, dtb):
    bsz, n, w = pab.shape
    tr = _tile(n, 512)
    spec = pl.BlockSpec((1, tr, w), lambda b, t: (b, t, 0))
    row = pl.BlockSpec((1, w), lambda b, t: (0, 0))
    return pl.pallas_call(
        _gdn_gate_kernel, grid=(bsz, n // tr), in_specs=[spec, row, row], out_specs=spec,
        out_shape=jax.ShapeDtypeStruct((bsz, n, w), F32),
        name="gdn_gates", compiler_params=_params(("parallel", "parallel")),
    )(pab, alog, dtb)


GDN_CHUNKS_PER_STEP = 2
GDN_REC_CHUNKS = 4


def _gdn_chunk_kernel(q_ref, k_ref, v_ref, gb_ref, gbt_ref, wq_ref, u_ref, lq_ref, eg_ref):
    cs, nh = DN_CHUNK, DN_HEADS
    r = lax.broadcasted_iota(jnp.int32, (cs, cs), 0)
    c = lax.broadcasted_iota(jnp.int32, (cs, cs), 1)
    eye = jnp.where(r == c, 1.0, 0.0)
    items = []
    for ci in range(GDN_CHUNKS_PER_STEP):
        rows = slice(ci * cs, (ci + 1) * cs)
        gb = gb_ref[0, rows]
        gbt = gbt_ref[0, ci]
        q, k, v = q_ref[0, rows], k_ref[0, rows], v_ref[0, rows]
        heads = []
        for h in range(nh):
            sl = slice(h * DN_DK, (h + 1) * DN_DK)
            kh, qh, vh = k[:, sl], q[:, sl], v[:, sl]
            kb = kh.astype(BF16)
            both = _dot_nt(jnp.concatenate([kb, qh.astype(BF16)], axis=0), kb)
            heads.append((kh, qh, vh, both[:cs], both[cs:], kh.T))
        for d in range(2):
            incl = (r >= c) if d == 0 else (r <= c)
            strict = (r > c) if d == 0 else (r < c)
            gcum_c = _dot3(jnp.where(incl, 1.0, 0.0), gb)
            gcum_r = _dot3(gbt, jnp.where(incl, 0.0, 1.0) + eye)
            for h in range(nh):
                kh, qh, vh, kk, qkt, kt = heads[h]
                col = d * nh + h
                gc = gcum_c[:, col:col + 1]
                gr = gcum_r[col:col + 1, :]
                bc = gb[:, 2 * nh + col:2 * nh + col + 1]
                glast = gc[cs - 1:cs, :] if d == 0 else gc[0:1, :]
                decay = jnp.where(incl, jnp.exp(jnp.where(incl, gc - gr, 0.0)), 0.0)
                egc = jnp.exp(gc)
                eg_ref[0, ci, d, h:h + 1, :] = jnp.broadcast_to(jnp.exp(glast), (1, DN_DV))
                wq_ref[0, ci, d, h, cs:2 * cs, :] = (qh * egc).astype(BF16)
                lq_ref[0, ci, d, h, 0:cs, :] = (qkt * decay).astype(BF16)
                lq_ref[0, ci, d, h, cs:, :] = (kt * jnp.exp(glast - gr)).astype(BF16)
                items.append(dict(ci=ci, d=d, h=h, a=jnp.where(strict, bc * kk * decay, 0.0),
                                  rhs=jnp.concatenate([vh * bc, kh * (bc * egc)], axis=1).astype(BF16)))
    masks = []
    for d in range(2):
        lo_r, lo_c = (r, c) if d == 0 else (c, r)
        ms = []
        s = 1
        while s < cs:
            ms.append(((lo_r // (2 * s)) == (lo_c // (2 * s))) & ((lo_r % (2 * s)) >= s) & ((lo_c % (2 * s)) < s))
            s *= 2
        masks.append(ms)
    for it in items:
        it["t"] = eye - jnp.where(masks[it["d"]][0], it["a"], 0.0)
    for lvl in range(1, len(masks[0])):
        for it in items:
            it["x"] = _dot(jnp.where(masks[it["d"]][lvl], it["a"], 0.0).astype(BF16), it["t"].astype(BF16))
        for it in items:
            it["t"] = it["t"] - _dot(it["t"].astype(BF16), it["x"].astype(BF16))
    for it in items:
        ah, al = _split3(it["a"])
        th, tl = _split3(it["t"])
        both = _dot(jnp.concatenate([ah, al], axis=0), th)
        it["res"] = eye - it["t"] - (both[:cs] + both[cs:] + _dot(ah, tl))
    for it in items:
        it["t"] = it["t"] + _dot(it["t"].astype(BF16), it["res"].astype(BF16))
    for it in items:
        uw = _dot(it["t"].astype(BF16), it["rhs"])
        ci, d, h = it["ci"], it["d"], it["h"]
        u_ref[0, ci, d, h] = uw[:, :DN_DV]
        wq_ref[0, ci, d, h, 0:cs, :] = uw[:, DN_DV:].astype(BF16)


def _gdn_chunks(qkv, gb, gbt):
    bsz, n, _ = qkv.shape
    cs = DN_CHUNK
    nc = n // cs
    cps = GDN_CHUNKS_PER_STEP
    w = DN_HEADS * DN_DK
    row = lambda col: pl.BlockSpec((1, cps * cs, w), lambda b, s, col=col: (b, s, col))

    def out(shape):
        return pl.BlockSpec((1, cps) + shape, lambda b, s: (b, s) + (0,) * len(shape))

    lead = (bsz, nc, 2, DN_HEADS)
    return pl.pallas_call(
        _gdn_chunk_kernel, grid=(bsz, nc // cps),
        in_specs=[row(0), row(1), row(2), pl.BlockSpec((1, cps * cs, LANES), lambda b, s: (b, s, 0)),
                  pl.BlockSpec((1, cps, LANES, cs), lambda b, s: (b, s, 0, 0))],
        out_specs=[out((2, DN_HEADS, 2 * cs, DN_DK)), out((2, DN_HEADS, cs, DN_DV)),
                   out((2, DN_HEADS, cs + DN_DK, cs)), out((2, DN_HEADS, DN_DV))],
        out_shape=[jax.ShapeDtypeStruct(lead + (2 * cs, DN_DK), BF16),
                   jax.ShapeDtypeStruct(lead + (cs, DN_DV), F32),
                   jax.ShapeDtypeStruct(lead + (cs + DN_DK, cs), BF16),
                   jax.ShapeDtypeStruct(lead + (DN_DV,), F32)],
        name="gdn_chunks", compiler_params=_params(("parallel", "parallel")),
    )(qkv, qkv, qkv, gb, gbt)


def _gdn_rec_kernel(wqf_ref, uf_ref, lqf_ref, egf_ref, wqb_ref, ub_ref, lqb_ref, egb_ref, s0_ref,
                    of_ref, ob_ref, st_ref):
    cs, nh = DN_CHUNK, DN_HEADS

    @pl.when(pl.program_id(1) == 0)
    def _():
        st_ref[...] = s0_ref[...]

    refs = ((wqf_ref, uf_ref, lqf_ref, egf_ref), (wqb_ref, ub_ref, lqb_ref, egb_ref))
    state = [[st_ref[0, d, h] for h in range(nh)] for d in range(2)]
    for j in range(GDN_REC_CHUNKS):
        chains = []
        for d in range(2):
            ci = j if d == 0 else GDN_REC_CHUNKS - 1 - j
            wq_ref, u_ref, lq_ref, eg_ref = refs[d]
            for h in range(nh):
                chains.append(dict(d=d, h=h, ci=ci, wq=wq_ref[0, ci, 0, h], u=u_ref[0, ci, 0, h],
                                   lq=lq_ref[0, ci, 0, h], eg=eg_ref[0, ci, 0, h:h + 1, :], s=state[d][h]))
        for ch in chains:
            ch["r1"] = _dot(ch["wq"], ch["s"].astype(BF16))
        for ch in chains:
            ch["vn"] = (ch["u"] - ch["r1"][:cs]).astype(BF16)
        for ch in chains:
            ch["r2"] = _dot(ch["lq"], ch["vn"])
        for ch in chains:
            d, h, ci = ch["d"], ch["h"], ch["ci"]
            o_ref = of_ref if d == 0 else ob_ref
            o_ref[0, ci * cs:(ci + 1) * cs, h * DN_DV:(h + 1) * DN_DV] = ch["r1"][cs:] + ch["r2"][:cs]
            state[d][h] = ch["s"] * ch["eg"] + ch["r2"][cs:]
    for d in range(2):
        for h in range(nh):
            st_ref[0, d, h] = state[d][h]


def _gdn_recurrence(wq, u, lq, eg, s0):
    bsz, nc = wq.shape[:2]
    cs = DN_CHUNK * GDN_REC_CHUNKS
    nb = nc // GDN_REC_CHUNKS
    w = DN_HEADS * DN_DK

    def spec(arr, d):
        shape = arr.shape[3:]
        zeros = (0,) * len(shape)
        if d == 0:
            return pl.BlockSpec((1, GDN_REC_CHUNKS, 1) + shape, lambda b, s: (b, s, 0) + zeros)
        return pl.BlockSpec((1, GDN_REC_CHUNKS, 1) + shape, lambda b, s: (b, nb - 1 - s, 1) + zeros)

    st_spec = pl.BlockSpec((1, 2, DN_HEADS, DN_DK, DN_DV), lambda b, s: (b, 0, 0, 0, 0))
    return pl.pallas_call(
        _gdn_rec_kernel, grid=(bsz, nb),
        in_specs=[spec(wq, 0), spec(u, 0), spec(lq, 0), spec(eg, 0),
                  spec(wq, 1), spec(u, 1), spec(lq, 1), spec(eg, 1), st_spec],
        out_specs=[pl.BlockSpec((1, cs, w), lambda b, s: (b, s, 0)),
                   pl.BlockSpec((1, cs, w), lambda b, s: (b, nb - 1 - s, 0)),
                   st_spec],
        out_shape=[jax.ShapeDtypeStruct((bsz, nb * cs, w), F32), jax.ShapeDtypeStruct((bsz, nb * cs, w), F32),
                   jax.ShapeDtypeStruct((bsz, 2, DN_HEADS, DN_DK, DN_DV), F32)],
        name="gdn_recurrence", compiler_params=_params(("parallel", "arbitrary")),
    )(wq, u, lq, eg, wq, u, lq, eg, s0)


def _gdn_post_kernel(of_ref, ob_ref, z_ref, g_ref, y_ref):
    o = of_ref[0] + ob_ref[0]
    z = z_ref[0]
    outs = []
    for h in range(DN_HEADS):
        sl = slice(h * DN_DV, (h + 1) * DN_DV)
        t = o[:, sl]
        outs.append(t * lax.rsqrt(jnp.mean(t * t, axis=-1, keepdims=True) + EPS) * g_ref[...])
    y_ref[0] = (jnp.concatenate(outs, axis=1) * _silu(z)).astype(y_ref.dtype)


def _gdn_post(o_pair, pd, g_norm):
    o_f, o_b = o_pair
    bsz, n, w = o_f.shape
    tr = _tile(n, 256)
    spec = pl.BlockSpec((1, tr, w), lambda b, t: (b, t, 0))
    return pl.pallas_call(
        _gdn_post_kernel, grid=(bsz, n // tr),
        in_specs=[spec, spec, pl.BlockSpec((1, tr, w), lambda b, t: (b, t, 3)),
                  pl.BlockSpec((1, DN_DV), lambda b, t: (0, 0))],
        out_specs=spec, out_shape=jax.ShapeDtypeStruct((bsz, n, w), BF16),
        name="gdn_post", compiler_params=_params(("parallel", "parallel")),
    )(o_f, o_b, pd, g_norm)


def _gdn_segment(pd, pab, conv_t, alog, dtb, s0):
    bsz, n, _ = pd.shape
    qkv = _gdn_prep(pd, conv_t)
    gb = _gdn_gates(pab, alog, dtb)
    gbt = jnp.swapaxes(gb.reshape(bsz, n // DN_CHUNK, DN_CHUNK, LANES), 2, 3)
    o_f, o_b, s_fin = _gdn_recurrence(*_gdn_chunks(qkv, gb, gbt), s0)
    return (o_f, o_b), s_fin


def _hy_prep_kernel(*refs):
    w_ref, u_ref, x0_ref = refs[9], refs[10], refs[11]
    w = w_ref[...]
    v = _short_conv(refs[0], refs[1], refs[2], w[0], HY_CONV)
    x1 = _short_conv(refs[3], refs[4], refs[5], w[1], HY_CONV)
    x0 = _short_conv(refs[6], refs[7], refs[8], w[2], HY_CONV)
    u_ref[0] = x1 * v
    x0_ref[0] = x0


def _hy_prep(ph, w3):
    bsz, n, _ = ph.shape
    tr = _tile(n, 256)
    tc = 512
    ncb = HY_WIDTH // tc
    specs = []
    for part in range(3):
        specs += _conv_specs(n, tr, tc, lambda c, part=part: part * ncb + c)
    out_spec = pl.BlockSpec((1, tr, tc), lambda b, t, c: (b, t, c))
    return pl.pallas_call(
        _hy_prep_kernel, grid=(bsz, n // tr, ncb), name="hy_prep",
        in_specs=specs + [pl.BlockSpec((3, 8, tc), lambda b, t, c: (0, 0, c))],
        out_specs=[out_spec, out_spec],
        out_shape=[jax.ShapeDtypeStruct((bsz, n, HY_WIDTH), F32)] * 2,
        compiler_params=_params(("parallel", "parallel", "parallel")),
    )(*([ph] * 9), w3)


DFT_BLK = 256


def _dft_tables(n):
    big = 2 * n
    k = jnp.arange(n, dtype=jnp.int32)[:, None]
    t = jnp.arange(n, dtype=jnp.int32)[None, :]
    ang = ((k * t) % big).astype(F32) * (2.0 * math.pi / big)
    f_re = jnp.cos(ang)
    f_im = jnp.where(k == 0, (1 - 2 * (t % 2)).astype(F32), -jnp.sin(ang))
    blocks = jnp.stack([f_re.reshape(n // DFT_BLK, DFT_BLK, n), f_im.reshape(n // DFT_BLK, DFT_BLK, n)], axis=1)
    return blocks.reshape(2 * n, n)


def _colabs_kernel(c_ref, o_ref):
    o_ref[...] = jnp.sum(jnp.sum(jnp.abs(c_ref[...]), axis=0), axis=0, keepdims=True)


def _coef_kernel(fh_ref, nrm_ref, o_ref, *, n):
    kidx = pl.program_id(0) * DFT_BLK + lax.broadcasted_iota(jnp.int32, fh_ref.shape[3:], 0)
    sgn = (1 - 2 * (kidx % 2)).astype(F32)
    first = kidx == 0
    fh = fh_ref[:, 0]
    p = fh[0, 0] + sgn * fh[1, 0]
    qraw = fh[0, 1] + jnp.where(first, 1.0, sgn) * fh[1, 1]
    dk = jnp.where(first, 1.0 / (2 * n), 2.0 / (2 * n)) * (1.0 / nrm_ref[...])
    q_m = jnp.where(first, 0.0, qraw)
    o_ref[0] = dk * p
    o_ref[1] = -dk * q_m
    o_ref[2] = dk * q_m
    o_ref[3] = dk * jnp.where(first, qraw, p)


def _hy_filter_coefs(n, f_tab, w1p, b1, w2p, b2, f0, f1, w3, feat, tcol, rates2):
    h = _mm(feat[None], w1p[None], x3=True, extras=[(b1, "row"), (f0, "row")],
            epilogue=lambda acc, b, f: jnp.sin(f * (acc + b)))
    h = _mm(h, w2p[None], x3=True, extras=[(b2, "row"), (f1, "row")],
            epilogue=lambda acc, b, f: jnp.sin(f * (acc + b)))
    h = _mm(h, w3[None], x3=True, extras=[(tcol, "col"), (rates2, "row")],
            epilogue=lambda acc, t, rt: acc * jnp.exp(-t * rt))[0]
    c_lo = h[:, :HY_WIDTH]
    hb = h[:, HY_WIDTH:]
    c_hi = jnp.concatenate([jnp.zeros((1, HY_WIDTH), F32), hb[:0:-1]], axis=0)
    c2 = jnp.stack([c_lo, c_hi])
    tc = 256
    nrm = pl.pallas_call(
        _colabs_kernel, grid=(HY_WIDTH // tc,),
        in_specs=[pl.BlockSpec((2, n, tc), lambda j: (0, 0, j))],
        out_specs=pl.BlockSpec((1, tc), lambda j: (0, j)),
        out_shape=jax.ShapeDtypeStruct((1, HY_WIDTH), F32),
        name="hy_filter_l1", compiler_params=_params(("parallel",)),
    )(c2)
    nb = n // DFT_BLK
    fh = _mm(f_tab[None], c2, x3=True, name="hy_filter_dft").reshape(2, nb, 2, DFT_BLK, HY_WIDTH)
    return pl.pallas_call(
        functools.partial(_coef_kernel, n=n), grid=(nb, HY_WIDTH // tc),
        in_specs=[pl.BlockSpec((2, 1, 2, DFT_BLK, tc), lambda t, j: (0, t, 0, 0, j)),
                  pl.BlockSpec((1, tc), lambda t, j: (0, j))],
        out_specs=pl.BlockSpec((4, DFT_BLK, tc), lambda t, j: (0, t, j)),
        out_shape=jax.ShapeDtypeStruct((4, n, HY_WIDTH), F32),
        name="hy_filter_coef", compiler_params=_params(("parallel", "parallel")),
    )(fh, nrm)


def _freq_mul(acc, cf):
    a, b = acc[:DFT_BLK], acc[DFT_BLK:]
    return jnp.concatenate([a * cf[0] + b * cf[1], a * cf[2] + b * cf[3]], axis=0)


def _hyena(ph, conv3, coef, f_tab, ft_tab, d_skip):
    bsz, n, _ = ph.shape
    u, x0 = _hy_prep(ph, conv3)
    coef_spec = lambda tm, tn: pl.BlockSpec((1, 4, DFT_BLK, tn), lambda gi, j, i, kk: (0, 0, i, j))
    y = _mm(f_tab[None], u, out_dtype=BF16, tm=2 * DFT_BLK, tn=HY_WIDTH, extras=[(coef[None], coef_spec)],
            epilogue=_freq_mul, name="hy_dft_filter")
    return _mm(ft_tab[None], y, out_dtype=BF16, tn=HY_WIDTH,
               extras=[(u, "tile"), (x0, "tile"), (d_skip, "row")],
               epilogue=lambda acc, uu, xx, dd: xx * (acc + dd * uu), name="hy_idft_gate")


def _merge_kernel(ya_ref, yd_ref, yh_ref, g0_ref, g1_ref, g2_ref, w_ref, o_ref):
    acc = _sigmoid(g0_ref[...]) * _dot(ya_ref[...], w_ref[0])
    acc = acc + _sigmoid(g1_ref[...]) * _dot(yd_ref[...], w_ref[1])
    acc = acc + _sigmoid(g2_ref[...]) * _dot(yh_ref[...], w_ref[2])
    o_ref[...] = acc.astype(o_ref.dtype)


def _merge(ya, yd, yh, pg, w_branch):
    m = ya.shape[0]
    tm = _tile(m, 512)
    tn = 512
    nj = D_MODEL // tn
    br = pl.BlockSpec((tm, ya.shape[1]), lambda j, i: (i, 0))
    gate = lambda r: pl.BlockSpec((tm, tn), lambda j, i, r=r: (i, r * nj + j))
    return pl.pallas_call(
        _merge_kernel, grid=(nj, m // tm), name="branch_merge",
        in_specs=[br, br, br, gate(0), gate(1), gate(2),
                  pl.BlockSpec((N_BRANCH, ya.shape[1], tn), lambda j, i: (0, 0, j))],
        out_specs=pl.BlockSpec((tm, tn), lambda j, i: (i, j)),
        out_shape=jax.ShapeDtypeStruct((m, D_MODEL), BF16),
        compiler_params=_params(("parallel", "parallel")),
    )(ya, yd, yh, pg, pg, pg, w_branch)


def _select_kernel(aff_ref, tri_ref, slot_ref, *, cap):
    aff = aff_ref[0]
    bits = pltpu.bitcast(aff, jnp.int32)
    thr = jnp.zeros((aff.shape[0], 1), jnp.int32)
    for bit in range(30, -1, -1):
        cand = thr | (1 << bit)
        cnt = jnp.sum(jnp.where(bits >= cand, 1.0, 0.0), axis=1, keepdims=True)
        thr = jnp.where(cnt >= cap, cand, thr)
    gt = bits > thr
    eq = bits == thr
    need = cap - jnp.sum(jnp.where(gt, 1.0, 0.0), axis=1, keepdims=True)
    tri = tri_ref[...]
    eq_before = _dot(jnp.where(eq, 1.0, 0.0).astype(BF16), tri)
    sel = gt | (eq & (eq_before < need))
    slot = _dot(jnp.where(sel, 1.0, 0.0).astype(BF16), tri)
    slot_ref[0] = jnp.where(sel, slot, -1.0).astype(jnp.int32)


def _select(aff_t, tri, cap):
    bsz, ne, n = aff_t.shape
    return pl.pallas_call(
        functools.partial(_select_kernel, cap=cap), grid=(bsz,), name="ffn_select",
        in_specs=[pl.BlockSpec((1, ne, n), lambda b: (b, 0, 0)), pl.BlockSpec((n, n), lambda b: (0, 0))],
        out_specs=pl.BlockSpec((1, ne, n), lambda b: (b, 0, 0)),
        out_shape=jax.ShapeDtypeStruct((bsz, ne, n), jnp.int32),
        compiler_params=_params(("parallel",)),
    )(aff_t, tri)


def _gather_kernel(slot_ref, aff_ref, h_ref, xe_ref, gs_ref, *, cap):
    e = pl.program_id(1)
    slot = slot_ref[0, pl.ds(e, 1), :]
    aff = aff_ref[0, pl.ds(e, 1), :]
    n = slot.shape[1]
    hit = lax.broadcasted_iota(jnp.int32, (cap, n), 0) == slot
    xe_ref[0, 0] = _dot(jnp.where(hit, 1.0, 0.0).astype(BF16), h_ref[0]).astype(xe_ref.dtype)
    gs_ref[0, 0] = jnp.sum(jnp.where(hit, aff, 0.0), axis=1, keepdims=True)


def _gather(slot, aff_t, h, cap):
    bsz, ne, n = slot.shape
    d = h.shape[2]
    row = pl.BlockSpec((1, ne, n), lambda b, e: (b, 0, 0))
    return pl.pallas_call(
        functools.partial(_gather_kernel, cap=cap), grid=(bsz, ne), name="ffn_gather",
        in_specs=[row, row, pl.BlockSpec((1, n, d), lambda b, e: (b, 0, 0))],
        out_specs=[pl.BlockSpec((1, 1, cap, d), lambda b, e: (e, b, 0, 0)),
                   pl.BlockSpec((1, 1, cap, 1), lambda b, e: (e, b, 0, 0))],
        out_shape=[jax.ShapeDtypeStruct((ne, bsz, cap, d), BF16),
                   jax.ShapeDtypeStruct((ne, bsz, cap, 1), F32)],
        compiler_params=_params(("parallel", "arbitrary")),
    )(slot, aff_t, h)


def _expert_kernel(x_ref, gs_ref, wg_ref, wu_ref, wd_ref, y_ref):
    x = x_ref[0]
    a = _dot(x, wg_ref[0])
    u = _dot(x, wu_ref[0])
    y = _dot((_silu(a) * u).astype(BF16), wd_ref[0])
    y_ref[0] = (y * gs_ref[0]).astype(y_ref.dtype)


def _experts(xe, gs, wg, wu, wd):
    ne, m, d = xe.shape
    ff = wg.shape[2]
    tm = _tile(m, 512)
    return pl.pallas_call(
        _expert_kernel, grid=(ne, m // tm), name="ffn_experts",
        in_specs=[pl.BlockSpec((1, tm, d), lambda e, i: (e, i, 0)),
                  pl.BlockSpec((1, tm, 1), lambda e, i: (e, i, 0)),
                  pl.BlockSpec((1, d, ff), lambda e, i: (e, 0, 0)),
                  pl.BlockSpec((1, d, ff), lambda e, i: (e, 0, 0)),
                  pl.BlockSpec((1, ff, d), lambda e, i: (e, 0, 0))],
        out_specs=pl.BlockSpec((1, tm, d), lambda e, i: (e, i, 0)),
        out_shape=jax.ShapeDtypeStruct((ne, m, d), BF16),
        compiler_params=_params(("parallel", "parallel")),
    )(xe, gs, wg, wu, wd)


def _scatter_kernel(slot_ref, ye_ref, x_ref, g_ref, o_ref, *, cap):
    ne = slot_ref.shape[1]
    tr = slot_ref.shape[2]
    lane = lax.broadcasted_iota(jnp.int32, (tr, cap), 1)
    hit = jnp.concatenate([jnp.where(lane == slot_ref[0, e], 1.0, 0.0).astype(BF16) for e in range(ne)],
                          axis=1)
    ye = ye_ref[...].reshape(ne * cap, ye_ref.shape[3])
    o_ref[0] = x_ref[0] + g_ref[0] * _dot(hit, ye)


def _scatter_residual(slot_col, ye, x, gate_row, cap):
    bsz, ne, n, _ = slot_col.shape
    d = x.shape[2]
    tr = _tile(n, 256)
    xs = pl.BlockSpec((1, tr, d), lambda b, i: (b, i, 0))
    return pl.pallas_call(
        functools.partial(_scatter_kernel, cap=cap), grid=(bsz, n // tr),
        in_specs=[pl.BlockSpec((1, ne, tr, 1), lambda b, i: (b, 0, i, 0)),
                  pl.BlockSpec((ne, 1, cap, d), lambda b, i: (0, b, 0, 0)),
                  xs, pl.BlockSpec((1, 1, d), lambda b, i: (b, 0, 0))],
        out_specs=xs, out_shape=jax.ShapeDtypeStruct((bsz, n, d), F32),
        name="ffn_scatter", compiler_params=_params(("parallel", "arbitrary")),
    )(slot_col, ye, x, gate_row)


def _ffn(x, gain, mod, router_t, wg, wu, wd, tri):
    bsz, n, d = x.shape
    cap = EC_CAPACITY * n // N_EXPERTS
    h, aff_t = _normmod(x, gain, mod, 3, 4, router_t)
    slot = _select(aff_t, tri, cap)
    xe, gs = _gather(slot, aff_t, h, cap)
    ye = _experts(xe.reshape(N_EXPERTS, bsz * cap, d), gs.reshape(N_EXPERTS, bsz * cap, 1), wg, wu, wd)
    return _scatter_residual(slot[..., None], ye.reshape(N_EXPERTS, bsz, cap, d), x, mod[:, 5:6], cap)


def _hy_consts(n):
    t = jnp.linspace(0.0, 1.0, n, dtype=F32)[:, None]
    pos = jnp.arange(n, dtype=F32)
    bands = jnp.linspace(1e-4, HY_BANDS - 1, HY_BANDS, dtype=F32)
    ang = (2.0 * math.pi / n) * pos[:, None] * bands[None, :]
    feat = jnp.concatenate([t, jnp.cos(ang), -jnp.sin(ang)], axis=-1)
    feat = jnp.pad(feat, ((0, 0), (0, LANES - HY_EMB)))
    f_tab = _dft_tables(n)
    return dict(feat=feat, tcol=t[None], f_tab=f_tab, f_bf=f_tab.astype(BF16),
                ft_bf=jnp.transpose(f_tab).astype(BF16))


def _mixer_segment_proj(h, weights):
    bsz, n, d = h.shape
    hf = h.reshape(1, bsz * n, d)
    return [None if w is None else
            _mm(hf, w[None], tm=1024, tn=1024, name="in_proj").reshape(bsz, n, w.shape[1]) for w in weights]


def kernel(x, c, ctx, c_ctx, w_ada, b_ada, norm_mix, norm_ffn, w_in, diff_lambda, diff_norm, dn_conv,
           dn_a_log, dn_dt_bias, dn_norm, hy_conv, hy_w1, hy_b1, hy_w2, hy_b2, hy_freq, hy_w3, hy_bias,
           w_branch, w_out, router, w_gate, w_up, w_down, norm_final):
    depth = w_ada.shape[0]
    bsz, n_lat, d = x.shape
    n_ctx = ctx.shape[1]
    assert d == D_MODEL and n_lat % DFT_BLK == 0 and n_ctx % DFT_BLK == 0 and DFT_BLK % DN_CHUNK == 0

    rope_tabs = _rope_tables(n_lat)
    hy_l = _hy_consts(n_lat)
    hy_c = _hy_consts(n_ctx)
    tri_l = (jnp.arange(n_lat)[:, None] < jnp.arange(n_lat)[None, :]).astype(BF16)
    tri_c = (jnp.arange(n_ctx)[:, None] < jnp.arange(n_ctx)[None, :]).astype(BF16)
    rates = jnp.abs(jnp.linspace(HY_DECAY_MIN, HY_DECAY_MAX, HY_WIDTH, dtype=F32))
    rates2 = jnp.concatenate([rates, rates])[None, None]
    cc = jnp.concatenate([c, jnp.broadcast_to(c_ctx[None], (8, d))], axis=0)[None]

    o_aq, o_dq, o_da, o_hy, o_gt = 0, 3072, 7168, 7200, 10272

    for l in range(depth):
        with_ctx = l < depth - 1
        lam0 = jnp.full((1, 1), 0.8 - 0.6 * math.exp(-0.3 * l), F32)
        mod = _mm(cc, w_ada[l][None], tn=1024, prologue=_silu, extras=[(b_ada[l][None, None], "row")],
                  epilogue=lambda acc, bias: acc + bias, name="ada_ln")[0]
        mod_l = jnp.pad(mod[:bsz].reshape(bsz, 6, d), ((0, 0), (0, 2), (0, 0)))
        mod_c = jnp.broadcast_to(jnp.pad(mod[bsz].reshape(1, 6, d), ((0, 0), (0, 2), (0, 0))), (bsz, 8, d))

        wl = w_in[l]
        w_attn = wl[:, o_aq:o_dq].astype(BF16)
        w_dn = wl[:, o_dq:o_da].astype(BF16)
        w_ab = jnp.pad(wl[:, o_da:o_hy], ((0, 0), (0, LANES - 4 * DN_HEADS))).astype(BF16)
        w_hy = wl[:, o_hy:o_gt].astype(BF16)
        w_gt = wl[:, o_gt:].astype(BF16)
        wb = w_branch[l].astype(BF16)
        wo = w_out[l].astype(BF16)
        wg, wu, wd = w_gate[l].astype(BF16), w_up[l].astype(BF16), w_down[l].astype(BF16)
        router_t = jnp.transpose(router[l])
        gain_mix, gain_ffn = norm_mix[l][None], norm_ffn[l][None]
        conv_dn = jnp.pad(jnp.transpose(dn_conv[l]), ((0, 8 - DN_CONV), (0, 0)))
        conv_hy = jnp.pad(jnp.transpose(hy_conv[l]), ((0, 8 - HY_CONV), (0, 0)))
        conv_hy = jnp.transpose(conv_hy.reshape(8, 3, HY_WIDTH), (1, 0, 2))
        alog = jnp.pad(dn_a_log[l].reshape(1, -1), ((0, 0), (0, LANES - 2 * DN_HEADS)))
        dtb = jnp.pad(dn_dt_bias[l].reshape(1, -1), ((0, 0), (0, LANES - 2 * DN_HEADS)))
        pad_h = lambda w: jnp.pad(w, ((0, LANES - w.shape[0]), (0, LANES - w.shape[1])))
        pad_r = lambda v: jnp.pad(v, (0, LANES - v.shape[0]))[None, None]
        w1p = pad_h(hy_w1[l])
        w2p = pad_h(hy_w2[l])
        w3p = jnp.pad(hy_w3[l], ((0, LANES - HY_HIDDEN), (0, 0)))
        filt = lambda n, hc: _hy_filter_coefs(
            n, hc["f_tab"], w1p, pad_r(hy_b1[l]), w2p, pad_r(hy_b2[l]), pad_r(hy_freq[l, 0]),
            pad_r(hy_freq[l, 1]), w3p, hc["feat"], hc["tcol"], rates2)
        d_skip = hy_bias[l][None, None]
        g_dn = dn_norm[l][None]
        g_diff = diff_norm[l][None]

        h_l = _normmod(x, gain_mix, mod_l, 0, 1)
        h_c = _normmod(ctx, gain_mix, mod_c, 0, 1)
        pa_l, pd_l, ph_l, pg_l, pab_l = _mixer_segment_proj(h_l, (w_attn, w_dn, w_hy, w_gt, w_ab))
        pa_c, pd_c, ph_c, pg_c, pab_c = _mixer_segment_proj(
            h_c, (w_attn, w_dn, w_hy if with_ctx else None, w_gt if with_ctx else None, w_ab))

        ya_l = _attention(pa_l, pa_c, diff_lambda[l], g_diff, lam0, rope_tabs)
        s0 = jnp.zeros((bsz, 2, DN_HEADS, DN_DK, DN_DV), F32)
        o_c, s_c = _gdn_segment(pd_c, pab_c, conv_dn, alog, dtb, s0)
        o_l, _ = _gdn_segment(pd_l, pab_l, conv_dn, alog, dtb, s_c)
        yd_l = _gdn_post(o_l, pd_l, g_dn)
        yh_l = _hyena(ph_l, conv_hy, filt(n_lat, hy_l), hy_l["f_bf"], hy_l["ft_bf"], d_skip)

        def merge_out(ya, yd, yh, pg, xs, mod_s):
            n = xs.shape[1]
            flat = lambda t: t.reshape(bsz * n, t.shape[2])
            y = _merge(flat(ya), flat(yd), flat(yh), flat(pg), wb).reshape(bsz, n, d)
            return _mm(y, wo[None], tm=1024, tn=1024, extras=[(xs, "tile"), (mod_s[:, 2:3], "row")],
                       epilogue=lambda acc, xx, gg: xx + gg * acc, name="out_proj_residual")

        x = merge_out(ya_l, yd_l, yh_l, pg_l, x, mod_l)
        x = _ffn(x, gain_ffn, mod_l, router_t, wg, wu, wd, tri_l)
        if with_ctx:
            ya_c = _attention(pa_c, None, diff_lambda[l], g_diff, lam0, None)
            yd_c = _gdn_post(o_c, pd_c, g_dn)
            yh_c = _hyena(ph_c, conv_hy, filt(n_ctx, hy_c), hy_c["f_bf"], hy_c["ft_bf"], d_skip)
            ctx = merge_out(ya_c, yd_c, yh_c, pg_c, ctx, mod_c)
            ctx = _ffn(ctx, gain_ffn, mod_c, router_t, wg, wu, wd, tri_c)
    return _final_norm(x, norm_final[None])
```

```python
import functools
import math

import jax
import jax.numpy as jnp
from jax import lax
from jax.experimental import pallas as pl
from jax.experimental.pallas import tpu as pltpu

F32 = jnp.float32
BF16 = jnp.bfloat16

D_MODEL = 2048
GRID_W = 64
EPS = 1e-6
A_HEADS = 8
A_DIM = 64
A_VDIM = 128
ROPE_BASE = 10000.0
DN_HEADS = 8
DN_DK = 128
DN_DV = 128
DN_CONV = 5
DN_CHUNK = 64
HY_WIDTH = 1024
HY_CONV = 3
HY_BANDS = 16
HY_EMB = 1 + 2 * HY_BANDS
HY_HIDDEN = 64
HY_DECAY_MIN = math.log(1e-2) / 1.5
HY_DECAY_MAX = math.log(1e-2) / 0.3
N_BRANCH = 3
N_EXPERTS = 16
EXPERT_FF = 1024
EC_CAPACITY = 2

LANES = 128
SUBLANES = 8
HALO = SUBLANES
VMEM_LIMIT = 56 * 1024 * 1024


def _params(sem, vmem=VMEM_LIMIT):
    return pltpu.CompilerParams(dimension_semantics=sem, vmem_limit_bytes=vmem)


def _tile(n, pref):
    if n <= pref:
        return n
    t = pref
    while n % t:
        t -= SUBLANES
    return t


def _split3(x):
    hi = x.astype(BF16)
    lo = (x - hi.astype(F32)).astype(BF16)
    return hi, lo


def _dot(a, b):
    return jnp.dot(a, b, preferred_element_type=F32)


def _dot3(a, b):
    ah, al = _split3(a)
    bh, bl = _split3(b)
    return _dot(ah, bh) + (_dot(ah, bl) + _dot(al, bh))


def _dot_nt(a, b):
    return lax.dot_general(a, b, (((1,), (1,)), ((), ())), preferred_element_type=F32)


def _dot3_nt(a, b):
    ah, al = _split3(a)
    bh, bl = _split3(b)
    return _dot_nt(ah, bh) + (_dot_nt(ah, bl) + _dot_nt(al, bh))


def _sigmoid(x):
    return 0.5 * (1.0 + jnp.tanh(0.5 * x))


def _silu(x):
    return x * _sigmoid(x)


def _mm_kernel(*refs, nk, n_extra, epilogue, prologue, x3):
    a_ref, b_ref = refs[0], refs[1]
    extra_refs = refs[2:2 + n_extra]
    o_ref = refs[2 + n_extra]
    a = a_ref[0]
    b = b_ref[0]
    if prologue is not None:
        a = prologue(a)
    if x3:
        part = _dot3(a.astype(F32), b.astype(F32))
    else:
        part = _dot(a.astype(BF16), b.astype(BF16))

    def finish(acc):
        if epilogue is not None:
            acc = epilogue(acc, *[r[0] for r in extra_refs])
        o_ref[0] = acc.astype(o_ref.dtype)

    if nk == 1:
        finish(part)
    else:
        acc_ref = refs[3 + n_extra]
        k = pl.program_id(3)

        @pl.when(k == 0)
        def _():
            acc_ref[...] = part

        @pl.when(k > 0)
        def _():
            acc_ref[...] += part

        @pl.when(k == nk - 1)
        def _():
            finish(acc_ref[...])


def _mm(a, b, *, out_dtype=F32, tm=512, tn=512, tk=None, extras=(), epilogue=None, prologue=None,
        x3=False, name="matmul"):
    ga, m, k = a.shape
    gb, kb, n = b.shape
    assert k == kb
    g = max(ga, gb)
    tm = _tile(m, tm)
    tn = n if n <= tn else tn
    tk = k if tk is None else min(tk, k)
    assert m % tm == 0 and n % tn == 0 and k % tk == 0, (m, n, k, tm, tn, tk)
    ni, nj, nk = m // tm, n // tn, k // tk
    in_specs = [
        pl.BlockSpec((1, tm, tk), lambda gi, j, i, kk: (gi if ga > 1 else 0, i, kk)),
        pl.BlockSpec((1, tk, tn), lambda gi, j, i, kk: (gi if gb > 1 else 0, kk, j)),
    ]
    args = [a, b]
    for arr, mode in extras:
        ge = arr.shape[0]
        if callable(mode):
            spec = mode(tm, tn)
        elif mode == "tile":
            spec = pl.BlockSpec((1, tm, tn), lambda gi, j, i, kk, ge=ge: (gi if ge > 1 else 0, i, j))
        elif mode == "row":
            spec = pl.BlockSpec((1, 1, tn), lambda gi, j, i, kk, ge=ge: (gi if ge > 1 else 0, 0, j))
        else:
            spec = pl.BlockSpec((1, tm, 1), lambda gi, j, i, kk, ge=ge: (gi if ge > 1 else 0, i, 0))
        in_specs.append(spec)
        args.append(arr)
    scratch = [pltpu.VMEM((tm, tn), F32)] if nk > 1 else []
    return pl.pallas_call(
        functools.partial(_mm_kernel, nk=nk, n_extra=len(extras), epilogue=epilogue,
                          prologue=prologue, x3=x3),
        grid=(g, nj, ni, nk),
        in_specs=in_specs,
        out_specs=pl.BlockSpec((1, tm, tn), lambda gi, j, i, kk: (gi, i, j)),
        out_shape=jax.ShapeDtypeStruct((g, m, n), out_dtype),
        scratch_shapes=scratch, name=name,
        compiler_params=_params(("parallel", "parallel", "parallel", "arbitrary")),
    )(*args)


def _normmod_body(x_ref, g_ref, mod_ref, shift_idx, scale_idx):
    x = x_ref[0]
    y = x * lax.rsqrt(jnp.mean(x * x, axis=-1, keepdims=True) + EPS) * g_ref[...]
    m = mod_ref[0]
    return y * (1.0 + m[scale_idx:scale_idx + 1]) + m[shift_idx:shift_idx + 1]


def _normmod_kernel(x_ref, g_ref, mod_ref, h_ref, *, shift_idx, scale_idx):
    h_ref[0] = _normmod_body(x_ref, g_ref, mod_ref, shift_idx, scale_idx).astype(h_ref.dtype)


def _normmod_router_kernel(x_ref, g_ref, mod_ref, rt_ref, h_ref, aff_ref, *, shift_idx, scale_idx):
    h = _normmod_body(x_ref, g_ref, mod_ref, shift_idx, scale_idx)
    h_ref[0] = h.astype(h_ref.dtype)
    logits = _dot3_nt(rt_ref[...], h)
    e = jnp.exp(logits - jnp.max(logits, axis=0, keepdims=True))
    aff_ref[0] = e * (1.0 / jnp.sum(e, axis=0, keepdims=True))


def _normmod(x, gain, mod, shift_idx, scale_idx, router_t=None):
    bsz, n, d = x.shape
    tr = _tile(n, 256)
    x_spec = pl.BlockSpec((1, tr, d), lambda b, t: (b, t, 0))
    g_spec = pl.BlockSpec((1, d), lambda b, t: (0, 0))
    m_spec = pl.BlockSpec((1, 8, d), lambda b, t: (b, 0, 0))
    if router_t is None:
        return pl.pallas_call(
            functools.partial(_normmod_kernel, shift_idx=shift_idx, scale_idx=scale_idx),
            grid=(bsz, n // tr), in_specs=[x_spec, g_spec, m_spec], out_specs=x_spec,
            out_shape=jax.ShapeDtypeStruct((bsz, n, d), BF16),
            name="normmod", compiler_params=_params(("parallel", "parallel")),
        )(x, gain, mod)
    ne = router_t.shape[0]
    return pl.pallas_call(
        functools.partial(_normmod_router_kernel, shift_idx=shift_idx, scale_idx=scale_idx),
        grid=(bsz, n // tr),
        in_specs=[x_spec, g_spec, m_spec, pl.BlockSpec((ne, d), lambda b, t: (0, 0))],
        out_specs=[x_spec, pl.BlockSpec((1, ne, tr), lambda b, t: (b, 0, t))],
        out_shape=[jax.ShapeDtypeStruct((bsz, n, d), BF16), jax.ShapeDtypeStruct((bsz, ne, n), F32)],
        name="normmod_router", compiler_params=_params(("parallel", "parallel")),
    )(x, gain, mod, router_t)


def _final_norm_kernel(x_ref, g_ref, o_ref):
    x = x_ref[0]
    o_ref[0] = x * lax.rsqrt(jnp.mean(x * x, axis=-1, keepdims=True) + EPS) * g_ref[...]


def _final_norm(x, gain):
    bsz, n, d = x.shape
    tr = _tile(n, 256)
    spec = pl.BlockSpec((1, tr, d), lambda b, t: (b, t, 0))
    return pl.pallas_call(
        _final_norm_kernel, grid=(bsz, n // tr), name="final_norm",
        in_specs=[spec, pl.BlockSpec((1, d), lambda b, t: (0, 0))], out_specs=spec,
        out_shape=jax.ShapeDtypeStruct((bsz, n, d), F32),
        compiler_params=_params(("parallel", "parallel")),
    )(x, gain)


def _rope_tables(n):
    t = jnp.arange(n, dtype=jnp.int32)
    row = (t // GRID_W).astype(F32)
    col = (t % GRID_W).astype(F32)
    n_freq = A_DIM // 4
    inv = ROPE_BASE ** (-jnp.arange(n_freq, dtype=F32) / n_freq)
    ang = jnp.concatenate([row[:, None] * inv, col[:, None] * inv], axis=-1)
    cos, sin = jnp.cos(ang), jnp.sin(ang)
    return jnp.tile(cos, (1, 4)), jnp.concatenate([-sin, sin, -sin, sin], axis=-1)


def _rope(x, c, s):
    lane = lax.broadcasted_iota(jnp.int32, x.shape, 1)
    partner = jnp.where((lane % A_DIM) < A_DIM // 2,
                        pltpu.roll(x, LANES - A_DIM // 2, axis=1), pltpu.roll(x, A_DIM // 2, axis=1))
    return x * c + partner * s


ATTN_Q_TILE = 2048
ATTN_Q_SUB = 256


def _attn_kernel(*refs, rope, with_ctx):
    it = iter(refs)
    q_ref, k_ref, v_ref = next(it), next(it), next(it)
    if with_ctx:
        kc_ref, vc_ref = next(it), next(it)
    if rope:
        cq_ref, sq_ref, ck_ref, sk_ref = next(it), next(it), next(it), next(it)
    lam_ref, lam0_ref, g_ref, o_ref = next(it), next(it), next(it), next(it)
    kb_s, va_s = next(it), next(it)

    @pl.when(pl.program_id(2) == 0)
    def _():
        k = k_ref[0]
        n = k.shape[0]
        if rope:
            k = _rope(k, ck_ref[...], sk_ref[...])
        kb_s[0:n] = k.astype(BF16)
        va_s[0:n, :A_VDIM] = v_ref[0].astype(BF16)
        if with_ctx:
            kb_s[n:] = kc_ref[0].astype(BF16)
            va_s[n:, :A_VDIM] = vc_ref[0].astype(BF16)
        va_s[:, A_VDIM:] = jnp.ones((va_s.shape[0], A_VDIM), BF16)

    q = q_ref[0]
    if rope:
        q = _rope(q, cq_ref[...], sq_ref[...])
    q = q * (A_DIM ** -0.5 * math.log2(math.e))
    lane = lax.broadcasted_iota(jnp.int32, q.shape, 1)
    lp = lam_ref[...]
    lam0 = lam0_ref[...]
    lam = (jnp.exp(jnp.sum(lp[0:1] * lp[1:2], axis=1, keepdims=True))
           - jnp.exp(jnp.sum(lp[2:3] * lp[3:4], axis=1, keepdims=True)) + lam0)
    kb = kb_s[...]
    tq = q.shape[0]
    sub = ATTN_Q_SUB if tq % ATTN_Q_SUB == 0 else tq
    chains = [(r0, half) for r0 in range(0, tq, sub) for half in (0, 1)]
    qhs = [jnp.where((lane // A_DIM) == half, q, 0.0).astype(BF16) for half in (0, 1)]
    st = {}

    def scores(c):
        st[c] = _dot_nt(qhs[c[1]][c[0]:c[0] + sub], kb)

    def softmax_pv(c):
        s = st[c]
        e = jnp.exp2(s - jnp.max(s, axis=-1, keepdims=True)).astype(BF16)
        r = _dot(e, va_s[...])
        st[c] = r[:, :A_VDIM] * (1.0 / r[:, A_VDIM:A_VDIM + 1])

    scores(chains[0])
    for j, c in enumerate(chains):
        if j + 1 < len(chains):
            scores(chains[j + 1])
        softmax_pv(c)
    rows = [st[(r0, 0)] - lam * st[(r0, 1)] for r0 in range(0, tq, sub)]
    o = jnp.concatenate(rows, axis=0) if len(rows) > 1 else rows[0]
    y = o * lax.rsqrt(jnp.mean(o * o, axis=-1, keepdims=True) + EPS) * g_ref[...]
    o_ref[0] = (y * (1.0 - lam0)).astype(o_ref.dtype)


def _attention(pa, pa_ctx, lam_p, g_norm, lam0, tables):
    bsz, n, _ = pa.shape
    tq = _tile(n, ATTN_Q_TILE)
    h_off = A_HEADS
    in_specs = [
        pl.BlockSpec((1, tq, LANES), lambda b, h, i: (b, i, h)),
        pl.BlockSpec((1, n, LANES), lambda b, h, i: (b, 0, h_off + h)),
        pl.BlockSpec((1, n, LANES), lambda b, h, i: (b, 0, 2 * h_off + h)),
    ]
    args = [pa, pa, pa]
    with_ctx = pa_ctx is not None
    nc = pa_ctx.shape[1] if with_ctx else 0
    scratch = [pltpu.VMEM((n + nc, LANES), BF16), pltpu.VMEM((n + nc, 2 * A_VDIM), BF16)]
    if with_ctx:
        in_specs += [pl.BlockSpec((1, nc, LANES), lambda b, h, i: (b, 0, h_off + h)),
                     pl.BlockSpec((1, nc, LANES), lambda b, h, i: (b, 0, 2 * h_off + h))]
        args += [pa_ctx, pa_ctx]
    rope = tables is not None
    if rope:
        ct, st = tables
        in_specs += [pl.BlockSpec((tq, LANES), lambda b, h, i: (i, 0))] * 2
        in_specs += [pl.BlockSpec((n, LANES), lambda b, h, i: (0, 0))] * 2
        args += [ct, st, ct, st]
    in_specs += [pl.BlockSpec((4, A_DIM), lambda b, h, i: (0, 0)),
                 pl.BlockSpec((1, 1), lambda b, h, i: (0, 0)),
                 pl.BlockSpec((1, A_VDIM), lambda b, h, i: (0, 0))]
    args += [lam_p, lam0, g_norm]
    return pl.pallas_call(
        functools.partial(_attn_kernel, rope=rope, with_ctx=with_ctx),
        grid=(bsz, A_HEADS, n // tq), in_specs=in_specs,
        out_specs=pl.BlockSpec((1, tq, LANES), lambda b, h, i: (b, i, h)),
        out_shape=jax.ShapeDtypeStruct((bsz, n, A_HEADS * A_VDIM), BF16),
        scratch_shapes=scratch, name="diff_attention",
        compiler_params=_params(("parallel", "parallel", "arbitrary")),
    )(*args)


def _conv_specs(n, tr, tc, col_of):
    r = tr // HALO
    last = n // HALO - 1
    return [
        pl.BlockSpec((1, HALO, tc), lambda b, t, c: (b, jnp.maximum(t * r - 1, 0), col_of(c))),
        pl.BlockSpec((1, tr, tc), lambda b, t, c: (b, t, col_of(c))),
        pl.BlockSpec((1, HALO, tc), lambda b, t, c: (b, jnp.minimum((t + 1) * r, last), col_of(c))),
    ]


def _short_conv(prev_ref, x_ref, next_ref, w, taps):
    t = pl.program_id(1)
    nt = pl.num_programs(1)
    x = x_ref[0]
    tr = x.shape[0]
    prev = jnp.where(t > 0, prev_ref[0], 0.0)
    nxt = jnp.where(t < nt - 1, next_ref[0], 0.0)
    xe = jnp.concatenate([prev, x, nxt], axis=0)
    rows = tr + 2 * HALO
    pad = (taps - 1) // 2
    acc = None
    for j in range(taps):
        d = j - pad
        sh = xe if d == 0 else pltpu.roll(xe, (rows - d) % rows, axis=0)
        term = sh[HALO:HALO + tr] * w[j:j + 1]
        acc = term if acc is None else acc + term
    return acc


def _gdn_prep_kernel(prev_ref, x_ref, next_ref, w_ref, o_ref):
    y = _silu(_short_conv(prev_ref, x_ref, next_ref, w_ref[...], DN_CONV))
    c = pl.program_id(2)
    outs = []
    for h in range(DN_HEADS):
        t = y[:, h * DN_DK:(h + 1) * DN_DK]
        rs = lax.rsqrt(jnp.sum(t * t, axis=-1, keepdims=True) + EPS)
        f = jnp.where(c == 0, rs * (DN_DK ** -0.5), jnp.where(c == 1, rs, 1.0))
        outs.append(t * f)
    o_ref[0] = jnp.concatenate(outs, axis=1)


def _gdn_prep(pd, w_t):
    bsz, n, _ = pd.shape
    tr = _tile(n, 256)
    tc = DN_HEADS * DN_DK
    specs = _conv_specs(n, tr, tc, lambda c: c)
    return pl.pallas_call(
        _gdn_prep_kernel, grid=(bsz, n // tr, 3),
        in_specs=specs + [pl.BlockSpec((8, tc), lambda b, t, c: (0, c))],
        out_specs=pl.BlockSpec((1, tr, tc), lambda b, t, c: (b, t, c)),
        out_shape=jax.ShapeDtypeStruct((bsz, n, 3 * tc), F32),
        name="gdn_prep", compiler_params=_params(("parallel", "parallel", "parallel")),
    )(pd, pd, pd, w_t)


def _gdn_gate_kernel(x_ref, alog_ref, dtb_ref, o_ref, ot_ref):
    x = x_ref[0]
    lane = lax.broadcasted_iota(jnp.int32, x.shape, 1)
    z = x + dtb_ref[...]
    softplus = jnp.maximum(z, 0.0) + jnp.log(1.0 + jnp.exp(-jnp.abs(z)))
    g = -jnp.exp(alog_ref[...]) * softplus
    gb = jnp.where(lane < 2 * DN_HEADS, g, _sigmoid(x))
    o_ref[0] = gb
    ot_ref[0] = gb.T


def _gdn_gates(pab, alog, dtb):
    bsz, n, w = pab.shape
    tr = _tile(n, 512)
    spec = pl.BlockSpec((1, tr, w), lambda b, t: (b, t, 0))
    row = pl.BlockSpec((1, w), lambda b, t: (0, 0))
    return pl.pallas_call(
        _gdn_gate_kernel, grid=(bsz, n // tr), in_specs=[spec, row, row],
        out_specs=[spec, pl.BlockSpec((1, w, tr), lambda b, t: (b, 0, t))],
        out_shape=[jax.ShapeDtypeStruct((bsz, n, w), F32), jax.ShapeDtypeStruct((bsz, w, n), F32)],
        name="gdn_gates", compiler_params=_params(("parallel", "parallel")),
    )(pab, alog, dtb)


GDN_CHUNKS_PER_STEP = 2
GDN_REC_CHUNKS = 4


def _gdn_chunk_kernel(q_ref, k_ref, v_ref, gb_ref, gbt_ref, wq_ref, u_ref, lq_ref, eg_ref):
    cs, nh = DN_CHUNK, DN_HEADS
    r = lax.broadcasted_iota(jnp.int32, (cs, cs), 0)
    c = lax.broadcasted_iota(jnp.int32, (cs, cs), 1)
    eye = jnp.where(r == c, 1.0, 0.0)
    items = []
    for ci in range(GDN_CHUNKS_PER_STEP):
        rows = slice(ci * cs, (ci + 1) * cs)
        gb = gb_ref[0, rows]
        gbt = gbt_ref[0, :, rows]
        q, k, v = q_ref[0, rows], k_ref[0, rows], v_ref[0, rows]
        heads = []
        for h in range(nh):
            sl = slice(h * DN_DK, (h + 1) * DN_DK)
            kh, qh, vh = k[:, sl], q[:, sl], v[:, sl]
            kb = kh.astype(BF16)
            both = _dot_nt(jnp.concatenate([kb, qh.astype(BF16)], axis=0), kb)
            heads.append((kh, qh, vh, both[:cs], both[cs:], kh.T))
        for d in range(2):
            incl = (r >= c) if d == 0 else (r <= c)
            strict = (r > c) if d == 0 else (r < c)
            gcum_c = _dot3(jnp.where(incl, 1.0, 0.0), gb)
            gcum_r = _dot3(gbt, jnp.where(incl, 0.0, 1.0) + eye)
            for h in range(nh):
                kh, qh, vh, kk, qkt, kt = heads[h]
                col = d * nh + h
                gc = gcum_c[:, col:col + 1]
                gr = gcum_r[col:col + 1, :]
                bc = gb[:, 2 * nh + col:2 * nh + col + 1]
                glast = gc[cs - 1:cs, :] if d == 0 else gc[0:1, :]
                decay = jnp.where(incl, jnp.exp(jnp.where(incl, gc - gr, 0.0)), 0.0)
                egc = jnp.exp(gc)
                eg_ref[0, ci, d, h:h + 1, :] = jnp.broadcast_to(jnp.exp(glast), (1, DN_DV))
                wq_ref[0, ci, d, h, cs:2 * cs, :] = (qh * egc).astype(BF16)
                lq_ref[0, ci, d, h, 0:cs, :] = (qkt * decay).astype(BF16)
                lq_ref[0, ci, d, h, cs:, :] = (kt * jnp.exp(glast - gr)).astype(BF16)
                items.append(dict(ci=ci, d=d, h=h, a=jnp.where(strict, bc * kk * decay, 0.0),
                                  rhs=jnp.concatenate([vh * bc, kh * (bc * egc)], axis=1).astype(BF16)))
    masks = []
    for d in range(2):
        lo_r, lo_c = (r, c) if d == 0 else (c, r)
        ms = []
        s = 1
        while s < cs:
            ms.append(((lo_r // (2 * s)) == (lo_c // (2 * s))) & ((lo_r % (2 * s)) >= s) & ((lo_c % (2 * s)) < s))
            s *= 2
        masks.append(ms)
    for it in items:
        it["t"] = eye - jnp.where(masks[it["d"]][0], it["a"], 0.0)
        it["ab"] = it["a"].astype(BF16)
    for lvl in range(1, len(masks[0])):
        for it in items:
            it["x"] = _dot(jnp.where(masks[it["d"]][lvl], it["ab"], jnp.zeros((), BF16)), it["t"].astype(BF16))
        for it in items:
            it["t"] = it["t"] - _dot(it["t"].astype(BF16), it["x"].astype(BF16))
    for it in items:
        ah, al = _split3(it["a"])
        th, tl = _split3(it["t"])
        both = _dot(jnp.concatenate([ah, al], axis=0), th)
        it["res"] = eye - it["t"] - (both[:cs] + both[cs:] + _dot(ah, tl))
    for it in items:
        it["t"] = it["t"] + _dot(it["t"].astype(BF16), it["res"].astype(BF16))
    for it in items:
        uw = _dot(it["t"].astype(BF16), it["rhs"])
        ci, d, h = it["ci"], it["d"], it["h"]
        u_ref[0, ci, d, h] = uw[:, :DN_DV]
        wq_ref[0, ci, d, h, 0:cs, :] = uw[:, DN_DV:].astype(BF16)


def _gdn_chunks(qkv, gb, gbt):
    bsz, n, _ = qkv.shape
    cs = DN_CHUNK
    nc = n // cs
    cps = GDN_CHUNKS_PER_STEP
    w = DN_HEADS * DN_DK
    row = lambda col: pl.BlockSpec((1, cps * cs, w), lambda b, s, col=col: (b, s, col))

    def out(shape):
        return pl.BlockSpec((1, cps) + shape, lambda b, s: (b, s) + (0,) * len(shape))

    lead = (bsz, nc, 2, DN_HEADS)
    return pl.pallas_call(
        _gdn_chunk_kernel, grid=(bsz, nc // cps),
        in_specs=[row(0), row(1), row(2), pl.BlockSpec((1, cps * cs, LANES), lambda b, s: (b, s, 0)),
                  pl.BlockSpec((1, LANES, cps * cs), lambda b, s: (b, 0, s))],
        out_specs=[out((2, DN_HEADS, 2 * cs, DN_DK)), out((2, DN_HEADS, cs, DN_DV)),
                   out((2, DN_HEADS, cs + DN_DK, cs)), out((2, DN_HEADS, DN_DV))],
        out_shape=[jax.ShapeDtypeStruct(lead + (2 * cs, DN_DK), BF16),
                   jax.ShapeDtypeStruct(lead + (cs, DN_DV), F32),
                   jax.ShapeDtypeStruct(lead + (cs + DN_DK, cs), BF16),
                   jax.ShapeDtypeStruct(lead + (DN_DV,), F32)],
        name="gdn_chunks", compiler_params=_params(("parallel", "parallel")),
    )(qkv, qkv, qkv, gb, gbt)


def _gdn_rec_kernel(wqf_ref, uf_ref, lqf_ref, egf_ref, wqb_ref, ub_ref, lqb_ref, egb_ref, s0_ref,
                    of_ref, ob_ref, st_ref):
    cs, nh = DN_CHUNK, DN_HEADS

    @pl.when(pl.program_id(1) == 0)
    def _():
        st_ref[...] = s0_ref[...]

    refs = ((wqf_ref, uf_ref, lqf_ref, egf_ref), (wqb_ref, ub_ref, lqb_ref, egb_ref))
    state = [[st_ref[0, d, h] for h in range(nh)] for d in range(2)]
    for j in range(GDN_REC_CHUNKS):
        chains = []
        for d in range(2):
            ci = j if d == 0 else GDN_REC_CHUNKS - 1 - j
            wq_ref, u_ref, lq_ref, eg_ref = refs[d]
            for h in range(nh):
                chains.append(dict(d=d, h=h, ci=ci, wq=wq_ref[0, ci, 0, h], u=u_ref[0, ci, 0, h],
                                   lq=lq_ref[0, ci, 0, h], eg=eg_ref[0, ci, 0, h:h + 1, :], s=state[d][h]))
        for ch in chains:
            ch["r1"] = _dot(ch["wq"], ch["s"].astype(BF16))
        for ch in chains:
            ch["vn"] = (ch["u"] - ch["r1"][:cs]).astype(BF16)
        for ch in chains:
            ch["r2"] = _dot(ch["lq"], ch["vn"])
        for ch in chains:
            d, h, ci = ch["d"], ch["h"], ch["ci"]
            o_ref = of_ref if d == 0 else ob_ref
            o_ref[0, ci * cs:(ci + 1) * cs, h * DN_DV:(h + 1) * DN_DV] = ch["r1"][cs:] + ch["r2"][:cs]
            state[d][h] = ch["s"] * ch["eg"] + ch["r2"][cs:]
    for d in range(2):
        for h in range(nh):
            st_ref[0, d, h] = state[d][h]


def _gdn_recurrence(wq, u, lq, eg, s0):
    bsz, nc = wq.shape[:2]
    cs = DN_CHUNK * GDN_REC_CHUNKS
    nb = nc // GDN_REC_CHUNKS
    w = DN_HEADS * DN_DK

    def spec(arr, d):
        shape = arr.shape[3:]
        zeros = (0,) * len(shape)
        if d == 0:
            return pl.BlockSpec((1, GDN_REC_CHUNKS, 1) + shape, lambda b, s: (b, s, 0) + zeros)
        return pl.BlockSpec((1, GDN_REC_CHUNKS, 1) + shape, lambda b, s: (b, nb - 1 - s, 1) + zeros)

    st_spec = pl.BlockSpec((1, 2, DN_HEADS, DN_DK, DN_DV), lambda b, s: (b, 0, 0, 0, 0))
    return pl.pallas_call(
        _gdn_rec_kernel, grid=(bsz, nb),
        in_specs=[spec(wq, 0), spec(u, 0), spec(lq, 0), spec(eg, 0),
                  spec(wq, 1), spec(u, 1), spec(lq, 1), spec(eg, 1), st_spec],
        out_specs=[pl.BlockSpec((1, cs, w), lambda b, s: (b, s, 0)),
                   pl.BlockSpec((1, cs, w), lambda b, s: (b, nb - 1 - s, 0)),
                   st_spec],
        out_shape=[jax.ShapeDtypeStruct((bsz, nb * cs, w), F32), jax.ShapeDtypeStruct((bsz, nb * cs, w), F32),
                   jax.ShapeDtypeStruct((bsz, 2, DN_HEADS, DN_DK, DN_DV), F32)],
        name="gdn_recurrence", compiler_params=_params(("parallel", "arbitrary")),
    )(wq, u, lq, eg, wq, u, lq, eg, s0)


def _gdn_post_kernel(of_ref, ob_ref, z_ref, g_ref, y_ref):
    o = of_ref[0] + ob_ref[0]
    z = z_ref[0]
    outs = []
    for h in range(DN_HEADS):
        sl = slice(h * DN_DV, (h + 1) * DN_DV)
        t = o[:, sl]
        outs.append(t * lax.rsqrt(jnp.mean(t * t, axis=-1, keepdims=True) + EPS) * g_ref[...])
    y_ref[0] = (jnp.concatenate(outs, axis=1) * _silu(z)).astype(y_ref.dtype)


def _gdn_post(o_pair, pd, g_norm):
    o_f, o_b = o_pair
    bsz, n, w = o_f.shape
    tr = _tile(n, 256)
    spec = pl.BlockSpec((1, tr, w), lambda b, t: (b, t, 0))
    return pl.pallas_call(
        _gdn_post_kernel, grid=(bsz, n // tr),
        in_specs=[spec, spec, pl.BlockSpec((1, tr, w), lambda b, t: (b, t, 3)),
                  pl.BlockSpec((1, DN_DV), lambda b, t: (0, 0))],
        out_specs=spec, out_shape=jax.ShapeDtypeStruct((bsz, n, w), BF16),
        name="gdn_post", compiler_params=_params(("parallel", "parallel")),
    )(o_f, o_b, pd, g_norm)


def _gdn_segment(pd, pab, conv_t, alog, dtb, s0):
    bsz, n, _ = pd.shape
    qkv = _gdn_prep(pd, conv_t)
    gb, gbt = _gdn_gates(pab, alog, dtb)
    o_f, o_b, s_fin = _gdn_recurrence(*_gdn_chunks(qkv, gb, gbt), s0)
    return (o_f, o_b), s_fin


def _hy_prep_kernel(*refs):
    w_ref, u_ref, x0_ref = refs[9], refs[10], refs[11]
    w = w_ref[...]
    v = _short_conv(refs[0], refs[1], refs[2], w[0], HY_CONV)
    x1 = _short_conv(refs[3], refs[4], refs[5], w[1], HY_CONV)
    x0 = _short_conv(refs[6], refs[7], refs[8], w[2], HY_CONV)
    u_ref[0] = x1 * v
    x0_ref[0] = x0


def _hy_prep(ph, w3):
    bsz, n, _ = ph.shape
    tr = _tile(n, 256)
    tc = 512
    ncb = HY_WIDTH // tc
    specs = []
    for part in range(3):
        specs += _conv_specs(n, tr, tc, lambda c, part=part: part * ncb + c)
    out_spec = pl.BlockSpec((1, tr, tc), lambda b, t, c: (b, t, c))
    return pl.pallas_call(
        _hy_prep_kernel, grid=(bsz, n // tr, ncb), name="hy_prep",
        in_specs=specs + [pl.BlockSpec((3, 8, tc), lambda b, t, c: (0, 0, c))],
        out_specs=[out_spec, out_spec],
        out_shape=[jax.ShapeDtypeStruct((bsz, n, HY_WIDTH), F32)] * 2,
        compiler_params=_params(("parallel", "parallel", "parallel")),
    )(*([ph] * 9), w3)


DFT_BLK = 256


def _dft_tables(n):
    big = 2 * n
    t = jnp.arange(n, dtype=jnp.int32)[None, :]
    angle = lambda k: ((k * t) % big).astype(F32) * (2.0 * math.pi / big)
    k0 = jnp.arange(DFT_BLK, dtype=jnp.int32)[:, None]
    k1 = jnp.arange(0, n, DFT_BLK, dtype=jnp.int32)[:, None]
    c0, s0 = jnp.cos(angle(k0))[None], jnp.sin(angle(k0))[None]
    c1, s1 = jnp.cos(angle(k1))[:, None], jnp.sin(angle(k1))[:, None]
    f_re = c1 * c0 - s1 * s0
    f_im = -(s1 * c0 + c1 * s0)
    nyquist = jnp.broadcast_to((1 - 2 * (t % 2)).astype(F32)[None], f_im.shape)
    f_im = jnp.where((k1[:, None] + k0[None]) == 0, nyquist, f_im)
    return jnp.stack([f_re, f_im], axis=1).reshape(2 * n, n)


def _colabs_kernel(c_ref, o_ref):
    o_ref[...] = jnp.sum(jnp.sum(jnp.abs(c_ref[...]), axis=0), axis=0, keepdims=True)


def _coef_kernel(fh_ref, nrm_ref, o_ref, *, n):
    kidx = pl.program_id(0) * DFT_BLK + lax.broadcasted_iota(jnp.int32, fh_ref.shape[3:], 0)
    sgn = (1 - 2 * (kidx % 2)).astype(F32)
    first = kidx == 0
    fh = fh_ref[:, 0]
    p = fh[0, 0] + sgn * fh[1, 0]
    qraw = fh[0, 1] + jnp.where(first, 1.0, sgn) * fh[1, 1]
    dk = jnp.where(first, 1.0 / (2 * n), 2.0 / (2 * n)) * (1.0 / nrm_ref[...])
    q_m = jnp.where(first, 0.0, qraw)
    o_ref[0] = dk * p
    o_ref[1] = -dk * q_m
    o_ref[2] = dk * q_m
    o_ref[3] = dk * jnp.where(first, qraw, p)


def _hy_filter_coefs(n, f_tab, w1p, b1, w2p, b2, f0, f1, w3, feat, tcol, rates2):
    h = _mm(feat[None], w1p[None], x3=True, extras=[(b1, "row"), (f0, "row")],
            epilogue=lambda acc, b, f: jnp.sin(f * (acc + b)))
    h = _mm(h, w2p[None], x3=True, extras=[(b2, "row"), (f1, "row")],
            epilogue=lambda acc, b, f: jnp.sin(f * (acc + b)))
    h = _mm(h, w3[None], x3=True, extras=[(tcol, "col"), (rates2, "row")],
            epilogue=lambda acc, t, rt: acc * jnp.exp(-t * rt))[0]
    c_lo = h[:, :HY_WIDTH]
    hb = h[:, HY_WIDTH:]
    c_hi = jnp.concatenate([jnp.zeros((1, HY_WIDTH), F32), hb[:0:-1]], axis=0)
    c2 = jnp.stack([c_lo, c_hi])
    tc = 256
    nrm = pl.pallas_call(
        _colabs_kernel, grid=(HY_WIDTH // tc,),
        in_specs=[pl.BlockSpec((2, n, tc), lambda j: (0, 0, j))],
        out_specs=pl.BlockSpec((1, tc), lambda j: (0, j)),
        out_shape=jax.ShapeDtypeStruct((1, HY_WIDTH), F32),
        name="hy_filter_l1", compiler_params=_params(("parallel",)),
    )(c2)
    nb = n // DFT_BLK
    fh = _mm(f_tab[None], c2, x3=True, name="hy_filter_dft").reshape(2, nb, 2, DFT_BLK, HY_WIDTH)
    return pl.pallas_call(
        functools.partial(_coef_kernel, n=n), grid=(nb, HY_WIDTH // tc),
        in_specs=[pl.BlockSpec((2, 1, 2, DFT_BLK, tc), lambda t, j: (0, t, 0, 0, j)),
                  pl.BlockSpec((1, tc), lambda t, j: (0, j))],
        out_specs=pl.BlockSpec((4, DFT_BLK, tc), lambda t, j: (0, t, j)),
        out_shape=jax.ShapeDtypeStruct((4, n, HY_WIDTH), F32),
        name="hy_filter_coef", compiler_params=_params(("parallel", "parallel")),
    )(fh, nrm)


def _freq_mul(acc, cf):
    a, b = acc[:DFT_BLK], acc[DFT_BLK:]
    return jnp.concatenate([a * cf[0] + b * cf[1], a * cf[2] + b * cf[3]], axis=0)


def _hyena(ph, conv3, coef, f_tab, ft_tab, d_skip):
    bsz, n, _ = ph.shape
    u, x0 = _hy_prep(ph, conv3)
    coef_spec = lambda tm, tn: pl.BlockSpec((1, 4, DFT_BLK, tn), lambda gi, j, i, kk: (0, 0, i, j))
    y = _mm(f_tab[None], u, out_dtype=BF16, tm=2 * DFT_BLK, tn=HY_WIDTH, extras=[(coef[None], coef_spec)],
            epilogue=_freq_mul, name="hy_dft_filter")
    return _mm(ft_tab[None], y, out_dtype=BF16, tn=HY_WIDTH,
               extras=[(u, "tile"), (x0, "tile"), (d_skip, "row")],
               epilogue=lambda acc, uu, xx, dd: xx * (acc + dd * uu), name="hy_idft_gate")


def _merge_kernel(ya_ref, yd_ref, yh_ref, g0_ref, g1_ref, g2_ref, w_ref, o_ref):
    acc = _sigmoid(g0_ref[...]) * _dot(ya_ref[...], w_ref[0])
    acc = acc + _sigmoid(g1_ref[...]) * _dot(yd_ref[...], w_ref[1])
    acc = acc + _sigmoid(g2_ref[...]) * _dot(yh_ref[...], w_ref[2])
    o_ref[...] = acc.astype(o_ref.dtype)


def _merge(ya, yd, yh, pg, w_branch):
    m = ya.shape[0]
    tm = _tile(m, 512)
    tn = 512
    nj = D_MODEL // tn
    br = pl.BlockSpec((tm, ya.shape[1]), lambda j, i: (i, 0))
    gate = lambda r: pl.BlockSpec((tm, tn), lambda j, i, r=r: (i, r * nj + j))
    return pl.pallas_call(
        _merge_kernel, grid=(nj, m // tm), name="branch_merge",
        in_specs=[br, br, br, gate(0), gate(1), gate(2),
                  pl.BlockSpec((N_BRANCH, ya.shape[1], tn), lambda j, i: (0, 0, j))],
        out_specs=pl.BlockSpec((tm, tn), lambda j, i: (i, j)),
        out_shape=jax.ShapeDtypeStruct((m, D_MODEL), BF16),
        compiler_params=_params(("parallel", "parallel")),
    )(ya, yd, yh, pg, pg, pg, w_branch)


def _select_kernel(aff_ref, tri_ref, slot_ref, *, cap):
    aff = aff_ref[0]
    bits = pltpu.bitcast(aff, jnp.int32)
    thr = jnp.zeros((aff.shape[0], 1), jnp.int32)
    for bit in range(30, -1, -1):
        cand = thr | (1 << bit)
        cnt = jnp.sum(jnp.where(bits >= cand, 1.0, 0.0), axis=1, keepdims=True)
        thr = jnp.where(cnt >= cap, cand, thr)
    gt = bits > thr
    eq = bits == thr
    need = cap - jnp.sum(jnp.where(gt, 1.0, 0.0), axis=1, keepdims=True)
    tri = tri_ref[...]
    eq_before = _dot(jnp.where(eq, 1.0, 0.0).astype(BF16), tri)
    sel = gt | (eq & (eq_before < need))
    slot = _dot(jnp.where(sel, 1.0, 0.0).astype(BF16), tri)
    slot_ref[0] = jnp.where(sel, slot, -1.0).astype(jnp.int32)


def _select(aff_t, tri, cap):
    bsz, ne, n = aff_t.shape
    return pl.pallas_call(
        functools.partial(_select_kernel, cap=cap), grid=(bsz,), name="ffn_select",
        in_specs=[pl.BlockSpec((1, ne, n), lambda b: (b, 0, 0)), pl.BlockSpec((n, n), lambda b: (0, 0))],
        out_specs=pl.BlockSpec((1, ne, n), lambda b: (b, 0, 0)),
        out_shape=jax.ShapeDtypeStruct((bsz, ne, n), jnp.int32),
        compiler_params=_params(("parallel",)),
    )(aff_t, tri)


def _gather_kernel(slot_ref, aff_ref, h_ref, xe_ref, gs_ref, *, cap):
    e = pl.program_id(1)
    slot = slot_ref[0, pl.ds(e, 1), :]
    aff = aff_ref[0, pl.ds(e, 1), :]
    n = slot.shape[1]
    hit = lax.broadcasted_iota(jnp.int32, (cap, n), 0) == slot
    xe_ref[0, 0] = _dot(jnp.where(hit, 1.0, 0.0).astype(BF16), h_ref[0]).astype(xe_ref.dtype)
    gs_ref[0, 0] = jnp.sum(jnp.where(hit, aff, 0.0), axis=1, keepdims=True)


def _gather(slot, aff_t, h, cap):
    bsz, ne, n = slot.shape
    d = h.shape[2]
    row = pl.BlockSpec((1, ne, n), lambda b, e: (b, 0, 0))
    return pl.pallas_call(
        functools.partial(_gather_kernel, cap=cap), grid=(bsz, ne), name="ffn_gather",
        in_specs=[row, row, pl.BlockSpec((1, n, d), lambda b, e: (b, 0, 0))],
        out_specs=[pl.BlockSpec((1, 1, cap, d), lambda b, e: (e, b, 0, 0)),
                   pl.BlockSpec((1, 1, cap, 1), lambda b, e: (e, b, 0, 0))],
        out_shape=[jax.ShapeDtypeStruct((ne, bsz, cap, d), BF16),
                   jax.ShapeDtypeStruct((ne, bsz, cap, 1), F32)],
        compiler_params=_params(("parallel", "arbitrary")),
    )(slot, aff_t, h)


def _expert_kernel(x_ref, gs_ref, wg_ref, wu_ref, wd_ref, y_ref):
    x = x_ref[0]
    a = _dot(x, wg_ref[0])
    u = _dot(x, wu_ref[0])
    y = _dot((_silu(a) * u).astype(BF16), wd_ref[0])
    y_ref[0] = (y * gs_ref[0]).astype(y_ref.dtype)


def _experts(xe, gs, wg, wu, wd):
    ne, m, d = xe.shape
    ff = wg.shape[2]
    tm = _tile(m, 512)
    return pl.pallas_call(
        _expert_kernel, grid=(ne, m // tm), name="ffn_experts",
        in_specs=[pl.BlockSpec((1, tm, d), lambda e, i: (e, i, 0)),
                  pl.BlockSpec((1, tm, 1), lambda e, i: (e, i, 0)),
                  pl.BlockSpec((1, d, ff), lambda e, i: (e, 0, 0)),
                  pl.BlockSpec((1, d, ff), lambda e, i: (e, 0, 0)),
                  pl.BlockSpec((1, ff, d), lambda e, i: (e, 0, 0))],
        out_specs=pl.BlockSpec((1, tm, d), lambda e, i: (e, i, 0)),
        out_shape=jax.ShapeDtypeStruct((ne, m, d), BF16),
        compiler_params=_params(("parallel", "parallel")),
    )(xe, gs, wg, wu, wd)


def _scatter_kernel(slot_ref, ye_ref, x_ref, g_ref, o_ref, *, cap):
    ne = slot_ref.shape[1]
    tr = slot_ref.shape[2]
    lane = lax.broadcasted_iota(jnp.int32, (tr, cap), 1)
    hit = jnp.concatenate([jnp.where(lane == slot_ref[0, e], 1.0, 0.0).astype(BF16) for e in range(ne)],
                          axis=1)
    ye = ye_ref[...].reshape(ne * cap, ye_ref.shape[3])
    o_ref[0] = x_ref[0] + g_ref[0] * _dot(hit, ye)


def _scatter_residual(slot_col, ye, x, gate_row, cap):
    bsz, ne, n, _ = slot_col.shape
    d = x.shape[2]
    tr = _tile(n, 256)
    xs = pl.BlockSpec((1, tr, d), lambda b, i: (b, i, 0))
    return pl.pallas_call(
        functools.partial(_scatter_kernel, cap=cap), grid=(bsz, n // tr),
        in_specs=[pl.BlockSpec((1, ne, tr, 1), lambda b, i: (b, 0, i, 0)),
                  pl.BlockSpec((ne, 1, cap, d), lambda b, i: (0, b, 0, 0)),
                  xs, pl.BlockSpec((1, 1, d), lambda b, i: (b, 0, 0))],
        out_specs=xs, out_shape=jax.ShapeDtypeStruct((bsz, n, d), F32),
        name="ffn_scatter", compiler_params=_params(("parallel", "arbitrary")),
    )(slot_col, ye, x, gate_row)


def _ffn(x, gain, mod, router_t, wg, wu, wd, tri):
    bsz, n, d = x.shape
    cap = EC_CAPACITY * n // N_EXPERTS
    h, aff_t = _normmod(x, gain, mod, 3, 4, router_t)
    slot = _select(aff_t, tri, cap)
    xe, gs = _gather(slot, aff_t, h, cap)
    ye = _experts(xe.reshape(N_EXPERTS, bsz * cap, d), gs.reshape(N_EXPERTS, bsz * cap, 1), wg, wu, wd)
    return _scatter_residual(slot[..., None], ye.reshape(N_EXPERTS, bsz, cap, d), x, mod[:, 5:6], cap)


def _hy_consts(n):
    t = jnp.linspace(0.0, 1.0, n, dtype=F32)[:, None]
    pos = jnp.arange(n, dtype=F32)
    bands = jnp.linspace(1e-4, HY_BANDS - 1, HY_BANDS, dtype=F32)
    ang = (2.0 * math.pi / n) * pos[:, None] * bands[None, :]
    feat = jnp.concatenate([t, jnp.cos(ang), -jnp.sin(ang)], axis=-1)
    feat = jnp.pad(feat, ((0, 0), (0, LANES - HY_EMB)))
    f_tab = _dft_tables(n)
    return dict(feat=feat, tcol=t[None], f_tab=f_tab, f_bf=f_tab.astype(BF16),
                ft_bf=jnp.transpose(f_tab).astype(BF16))


PROJ_GROUPS = ((0, 3072), (3072, 4096), (7168, 32), (7200, 3072), (10272, 6144))


def _in_proj_weights_kernel(w_ref, *outs):
    for (off, width), o_ref in zip(PROJ_GROUPS, outs):
        blk = w_ref[0, :, off:off + width].astype(BF16)
        if width < o_ref.shape[1]:
            blk = jnp.concatenate([blk, jnp.zeros((blk.shape[0], o_ref.shape[1] - width), BF16)], axis=1)
        o_ref[...] = blk


def _in_proj_weights(w_in, layer):
    _, d, total = w_in.shape
    tr = 256
    widths = [max(w, LANES) for _, w in PROJ_GROUPS]
    return pl.pallas_call(
        _in_proj_weights_kernel, grid=(d // tr,),
        in_specs=[pl.BlockSpec((1, tr, total), lambda i: (layer, i, 0))],
        out_specs=[pl.BlockSpec((tr, w), lambda i: (i, 0)) for w in widths],
        out_shape=[jax.ShapeDtypeStruct((d, w), BF16) for w in widths],
        name="in_proj_weights", compiler_params=_params(("parallel",)),
    )(w_in)


def _mixer_segment_proj(h, weights):
    bsz, n, d = h.shape
    hf = h.reshape(1, bsz * n, d)
    return [None if w is None else
            _mm(hf, w[None], tm=1024, tn=1024, name="in_proj").reshape(bsz, n, w.shape[1]) for w in weights]


def kernel(x, c, ctx, c_ctx, w_ada, b_ada, norm_mix, norm_ffn, w_in, diff_lambda, diff_norm, dn_conv,
           dn_a_log, dn_dt_bias, dn_norm, hy_conv, hy_w1, hy_b1, hy_w2, hy_b2, hy_freq, hy_w3, hy_bias,
           w_branch, w_out, router, w_gate, w_up, w_down, norm_final):
    depth = w_ada.shape[0]
    bsz, n_lat, d = x.shape
    n_ctx = ctx.shape[1]
    assert d == D_MODEL and n_lat % DFT_BLK == 0 and n_ctx % DFT_BLK == 0 and DFT_BLK % DN_CHUNK == 0

    rope_tabs = _rope_tables(n_lat)
    hy_l = _hy_consts(n_lat)
    hy_c = _hy_consts(n_ctx)
    tri_l = (jnp.arange(n_lat)[:, None] < jnp.arange(n_lat)[None, :]).astype(BF16)
    tri_c = (jnp.arange(n_ctx)[:, None] < jnp.arange(n_ctx)[None, :]).astype(BF16)
    rates = jnp.abs(jnp.linspace(HY_DECAY_MIN, HY_DECAY_MAX, HY_WIDTH, dtype=F32))
    rates2 = jnp.concatenate([rates, rates])[None, None]
    cc = jnp.concatenate([c, jnp.broadcast_to(c_ctx[None], (8, d))], axis=0)[None]

    for l in range(depth):
        with_ctx = l < depth - 1
        lam0 = jnp.full((1, 1), 0.8 - 0.6 * math.exp(-0.3 * l), F32)
        mod = _mm(cc, w_ada[l][None], tn=1024, prologue=_silu, extras=[(b_ada[l][None, None], "row")],
                  epilogue=lambda acc, bias: acc + bias, name="ada_ln")[0]
        mod_l = jnp.pad(mod[:bsz].reshape(bsz, 6, d), ((0, 0), (0, 2), (0, 0)))
        mod_c = jnp.broadcast_to(jnp.pad(mod[bsz].reshape(1, 6, d), ((0, 0), (0, 2), (0, 0))), (bsz, 8, d))

        w_attn, w_dn, w_ab, w_hy, w_gt = _in_proj_weights(w_in, l)
        wb = w_branch[l].astype(BF16)
        wo = w_out[l].astype(BF16)
        wg, wu, wd = w_gate[l].astype(BF16), w_up[l].astype(BF16), w_down[l].astype(BF16)
        router_t = jnp.transpose(router[l])
        gain_mix, gain_ffn = norm_mix[l][None], norm_ffn[l][None]
        conv_dn = jnp.pad(jnp.transpose(dn_conv[l]), ((0, 8 - DN_CONV), (0, 0)))
        conv_hy = jnp.pad(jnp.transpose(hy_conv[l]), ((0, 8 - HY_CONV), (0, 0)))
        conv_hy = jnp.transpose(conv_hy.reshape(8, 3, HY_WIDTH), (1, 0, 2))
        alog = jnp.pad(dn_a_log[l].reshape(1, -1), ((0, 0), (0, LANES - 2 * DN_HEADS)))
        dtb = jnp.pad(dn_dt_bias[l].reshape(1, -1), ((0, 0), (0, LANES - 2 * DN_HEADS)))
        pad_h = lambda w: jnp.pad(w, ((0, LANES - w.shape[0]), (0, LANES - w.shape[1])))
        pad_r = lambda v: jnp.pad(v, (0, LANES - v.shape[0]))[None, None]
        w1p = pad_h(hy_w1[l])
        w2p = pad_h(hy_w2[l])
        w3p = jnp.pad(hy_w3[l], ((0, LANES - HY_HIDDEN), (0, 0)))
        filt = lambda n, hc: _hy_filter_coefs(
            n, hc["f_tab"], w1p, pad_r(hy_b1[l]), w2p, pad_r(hy_b2[l]), pad_r(hy_freq[l, 0]),
            pad_r(hy_freq[l, 1]), w3p, hc["feat"], hc["tcol"], rates2)
        d_skip = hy_bias[l][None, None]
        g_dn = dn_norm[l][None]
        g_diff = diff_norm[l][None]

        h_l = _normmod(x, gain_mix, mod_l, 0, 1)
        h_c = _normmod(ctx, gain_mix, mod_c, 0, 1)
        pa_l, pd_l, ph_l, pg_l, pab_l = _mixer_segment_proj(h_l, (w_attn, w_dn, w_hy, w_gt, w_ab))
        pa_c, pd_c, ph_c, pg_c, pab_c = _mixer_segment_proj(
            h_c, (w_attn, w_dn, w_hy if with_ctx else None, w_gt if with_ctx else None, w_ab))

        ya_l = _attention(pa_l, pa_c, diff_lambda[l], g_diff, lam0, rope_tabs)
        s0 = jnp.zeros((bsz, 2, DN_HEADS, DN_DK, DN_DV), F32)
        o_c, s_c = _gdn_segment(pd_c, pab_c, conv_dn, alog, dtb, s0)
        o_l, _ = _gdn_segment(pd_l, pab_l, conv_dn, alog, dtb, s_c)
        yd_l = _gdn_post(o_l, pd_l, g_dn)
        yh_l = _hyena(ph_l, conv_hy, filt(n_lat, hy_l), hy_l["f_bf"], hy_l["ft_bf"], d_skip)

        def merge_out(ya, yd, yh, pg, xs, mod_s):
            n = xs.shape[1]
            flat = lambda t: t.reshape(bsz * n, t.shape[2])
            y = _merge(flat(ya), flat(yd), flat(yh), flat(pg), wb).reshape(bsz, n, d)
            return _mm(y, wo[None], tm=1024, tn=1024, extras=[(xs, "tile"), (mod_s[:, 2:3], "row")],
                       epilogue=lambda acc, xx, gg: xx + gg * acc, name="out_proj_residual")

        x = merge_out(ya_l, yd_l, yh_l, pg_l, x, mod_l)
        x = _ffn(x, gain_ffn, mod_l, router_t, wg, wu, wd, tri_l)
        if with_ctx:
            ya_c = _attention(pa_c, None, diff_lambda[l], g_diff, lam0, None)
            yd_c = _gdn_post(o_c, pd_c, g_dn)
            yh_c = _hyena(ph_c, conv_hy, filt(n_ctx, hy_c), hy_c["f_bf"], hy_c["ft_bf"], d_skip)
            ctx = merge_out(ya_c, yd_c, yh_c, pg_c, ctx, mod_c)
            ctx = _ffn(ctx, gain_ffn, mod_c, router_t, wg, wu, wd, tri_c)
    return _final_norm(x, norm_final[None])
```

```python
import functools
import math

import jax
import jax.numpy as jnp
from jax import lax
from jax.experimental import pallas as pl
from jax.experimental.pallas import tpu as pltpu

F32 = jnp.float32
BF16 = jnp.bfloat16

D_MODEL = 2048
GRID_W = 64
EPS = 1e-6
A_HEADS = 8
A_DIM = 64
A_VDIM = 128
ROPE_BASE = 10000.0
DN_HEADS = 8
DN_DK = 128
DN_DV = 128
DN_CONV = 5
DN_CHUNK = 64
HY_WIDTH = 1024
HY_CONV = 3
HY_BANDS = 16
HY_EMB = 1 + 2 * HY_BANDS
HY_HIDDEN = 64
HY_DECAY_MIN = math.log(1e-2) / 1.5
HY_DECAY_MAX = math.log(1e-2) / 0.3
N_BRANCH = 3
N_EXPERTS = 16
EXPERT_FF = 1024
EC_CAPACITY = 2

LANES = 128
SUBLANES = 8
HALO = SUBLANES
VMEM_LIMIT = 56 * 1024 * 1024


def _params(sem, vmem=VMEM_LIMIT):
    return pltpu.CompilerParams(dimension_semantics=sem, vmem_limit_bytes=vmem)


def _tile(n, pref):
    if n <= pref:
        return n
    t = pref
    while n % t:
        t -= SUBLANES
    return t


def _split3(x):
    hi = x.astype(BF16)
    lo = (x - hi.astype(F32)).astype(BF16)
    return hi, lo


def _dot(a, b):
    return jnp.dot(a, b, preferred_element_type=F32)


def _dot3(a, b):
    ah, al = _split3(a)
    bh, bl = _split3(b)
    return _dot(ah, bh) + (_dot(ah, bl) + _dot(al, bh))


def _dot_nt(a, b):
    return lax.dot_general(a, b, (((1,), (1,)), ((), ())), preferred_element_type=F32)


def _dot3_nt(a, b):
    ah, al = _split3(a)
    bh, bl = _split3(b)
    return _dot_nt(ah, bh) + (_dot_nt(ah, bl) + _dot_nt(al, bh))


def _sigmoid(x):
    return 0.5 * (1.0 + jnp.tanh(0.5 * x))


def _silu(x):
    return x * _sigmoid(x)


def _mm_kernel(*refs, nk, n_extra, epilogue, prologue, x3):
    a_ref, b_ref = refs[0], refs[1]
    extra_refs = refs[2:2 + n_extra]
    o_ref = refs[2 + n_extra]
    a = a_ref[0]
    b = b_ref[0]
    if prologue is not None:
        a = prologue(a)
    if x3:
        part = _dot3(a.astype(F32), b.astype(F32))
    else:
        part = _dot(a.astype(BF16), b.astype(BF16))

    def finish(acc):
        if epilogue is not None:
            acc = epilogue(acc, *[r[0] for r in extra_refs])
        o_ref[0] = acc.astype(o_ref.dtype)

    if nk == 1:
        finish(part)
    else:
        acc_ref = refs[3 + n_extra]
        k = pl.program_id(3)

        @pl.when(k == 0)
        def _():
            acc_ref[...] = part

        @pl.when(k > 0)
        def _():
            acc_ref[...] += part

        @pl.when(k == nk - 1)
        def _():
            finish(acc_ref[...])


def _mm(a, b, *, out_dtype=F32, tm=512, tn=512, tk=None, extras=(), epilogue=None, prologue=None,
        x3=False, b_sel=None, name="matmul"):
    ga, m, k = a.shape
    gb, kb, n = b.shape
    assert k == kb
    if b_sel is not None:
        gb = 1
    g = max(ga, gb)
    tm = _tile(m, tm)
    tn = n if n <= tn else tn
    tk = k if tk is None else min(tk, k)
    assert m % tm == 0 and n % tn == 0 and k % tk == 0, (m, n, k, tm, tn, tk)
    ni, nj, nk = m // tm, n // tn, k // tk
    in_specs = [
        pl.BlockSpec((1, tm, tk), lambda gi, j, i, kk: (gi if ga > 1 else 0, i, kk)),
        pl.BlockSpec((1, tk, tn), lambda gi, j, i, kk: (gi if gb > 1 else (b_sel or 0), kk, j)),
    ]
    args = [a, b]
    for arr, mode in extras:
        ge = arr.shape[0]
        if callable(mode):
            spec = mode(tm, tn)
        elif mode == "tile":
            spec = pl.BlockSpec((1, tm, tn), lambda gi, j, i, kk, ge=ge: (gi if ge > 1 else 0, i, j))
        elif mode == "row":
            spec = pl.BlockSpec((1, 1, tn), lambda gi, j, i, kk, ge=ge: (gi if ge > 1 else 0, 0, j))
        else:
            spec = pl.BlockSpec((1, tm, 1), lambda gi, j, i, kk, ge=ge: (gi if ge > 1 else 0, i, 0))
        in_specs.append(spec)
        args.append(arr)
    scratch = [pltpu.VMEM((tm, tn), F32)] if nk > 1 else []
    return pl.pallas_call(
        functools.partial(_mm_kernel, nk=nk, n_extra=len(extras), epilogue=epilogue,
                          prologue=prologue, x3=x3),
        grid=(g, nj, ni, nk),
        in_specs=in_specs,
        out_specs=pl.BlockSpec((1, tm, tn), lambda gi, j, i, kk: (gi, i, j)),
        out_shape=jax.ShapeDtypeStruct((g, m, n), out_dtype),
        scratch_shapes=scratch, name=name,
        compiler_params=_params(("parallel", "parallel", "parallel", "arbitrary")),
    )(*args)


def _normmod_body(x_ref, g_ref, mod_ref, shift_idx, scale_idx):
    x = x_ref[0]
    y = x * lax.rsqrt(jnp.mean(x * x, axis=-1, keepdims=True) + EPS) * g_ref[...]
    m = mod_ref[0]
    return y * (1.0 + m[scale_idx:scale_idx + 1]) + m[shift_idx:shift_idx + 1]


def _normmod_kernel(x_ref, g_ref, mod_ref, h_ref, *, shift_idx, scale_idx):
    h_ref[0] = _normmod_body(x_ref, g_ref, mod_ref, shift_idx, scale_idx).astype(h_ref.dtype)


def _normmod_router_kernel(x_ref, g_ref, mod_ref, rt_ref, h_ref, aff_ref, *, shift_idx, scale_idx):
    h = _normmod_body(x_ref, g_ref, mod_ref, shift_idx, scale_idx)
    h_ref[0] = h.astype(h_ref.dtype)
    logits = _dot3_nt(rt_ref[...], h)
    e = jnp.exp(logits - jnp.max(logits, axis=0, keepdims=True))
    aff_ref[0] = e * (1.0 / jnp.sum(e, axis=0, keepdims=True))


def _normmod(x, gain, mod, shift_idx, scale_idx, router_t=None):
    bsz, n, d = x.shape
    tr = _tile(n, 256)
    x_spec = pl.BlockSpec((1, tr, d), lambda b, t: (b, t, 0))
    g_spec = pl.BlockSpec((1, d), lambda b, t: (0, 0))
    m_spec = pl.BlockSpec((1, 8, d), lambda b, t: (b, 0, 0))
    if router_t is None:
        return pl.pallas_call(
            functools.partial(_normmod_kernel, shift_idx=shift_idx, scale_idx=scale_idx),
            grid=(bsz, n // tr), in_specs=[x_spec, g_spec, m_spec], out_specs=x_spec,
            out_shape=jax.ShapeDtypeStruct((bsz, n, d), BF16),
            name="normmod", compiler_params=_params(("parallel", "parallel")),
        )(x, gain, mod)
    ne = router_t.shape[0]
    return pl.pallas_call(
        functools.partial(_normmod_router_kernel, shift_idx=shift_idx, scale_idx=scale_idx),
        grid=(bsz, n // tr),
        in_specs=[x_spec, g_spec, m_spec, pl.BlockSpec((ne, d), lambda b, t: (0, 0))],
        out_specs=[x_spec, pl.BlockSpec((1, ne, tr), lambda b, t: (b, 0, t))],
        out_shape=[jax.ShapeDtypeStruct((bsz, n, d), BF16), jax.ShapeDtypeStruct((bsz, ne, n), F32)],
        name="normmod_router", compiler_params=_params(("parallel", "parallel")),
    )(x, gain, mod, router_t)


def _final_norm_kernel(x_ref, g_ref, o_ref):
    x = x_ref[0]
    o_ref[0] = x * lax.rsqrt(jnp.mean(x * x, axis=-1, keepdims=True) + EPS) * g_ref[...]


def _final_norm(x, gain):
    bsz, n, d = x.shape
    tr = _tile(n, 256)
    spec = pl.BlockSpec((1, tr, d), lambda b, t: (b, t, 0))
    return pl.pallas_call(
        _final_norm_kernel, grid=(bsz, n // tr), name="final_norm",
        in_specs=[spec, pl.BlockSpec((1, d), lambda b, t: (0, 0))], out_specs=spec,
        out_shape=jax.ShapeDtypeStruct((bsz, n, d), F32),
        compiler_params=_params(("parallel", "parallel")),
    )(x, gain)


def _rope_tables(n):
    t = jnp.arange(n, dtype=jnp.int32)
    row = (t // GRID_W).astype(F32)
    col = (t % GRID_W).astype(F32)
    n_freq = A_DIM // 4
    inv = ROPE_BASE ** (-jnp.arange(n_freq, dtype=F32) / n_freq)
    ang = jnp.concatenate([row[:, None] * inv, col[:, None] * inv], axis=-1)
    cos, sin = jnp.cos(ang), jnp.sin(ang)
    return jnp.tile(cos, (1, 4)), jnp.concatenate([-sin, sin, -sin, sin], axis=-1)


def _rope(x, c, s):
    lane = lax.broadcasted_iota(jnp.int32, x.shape, 1)
    partner = jnp.where((lane % A_DIM) < A_DIM // 2,
                        pltpu.roll(x, LANES - A_DIM // 2, axis=1), pltpu.roll(x, A_DIM // 2, axis=1))
    return x * c + partner * s


ATTN_Q_TILE = 2048
ATTN_Q_SUB = 256


def _attn_kernel(*refs, rope, with_ctx):
    it = iter(refs)
    q_ref, k_ref, v_ref = next(it), next(it), next(it)
    if with_ctx:
        kc_ref, vc_ref = next(it), next(it)
    if rope:
        cq_ref, sq_ref, ck_ref, sk_ref = next(it), next(it), next(it), next(it)
    lam_ref, lam0_ref, g_ref, o_ref = next(it), next(it), next(it), next(it)
    kb_s, va_s = next(it), next(it)

    @pl.when(pl.program_id(2) == 0)
    def _():
        k = k_ref[0]
        n = k.shape[0]
        if rope:
            k = _rope(k, ck_ref[...], sk_ref[...])
        kb_s[0:n] = k.astype(BF16)
        va_s[0:n, :A_VDIM] = v_ref[0].astype(BF16)
        if with_ctx:
            kb_s[n:] = kc_ref[0].astype(BF16)
            va_s[n:, :A_VDIM] = vc_ref[0].astype(BF16)
        va_s[:, A_VDIM:] = jnp.ones((va_s.shape[0], A_VDIM), BF16)

    q = q_ref[0]
    if rope:
        q = _rope(q, cq_ref[...], sq_ref[...])
    q = q * (A_DIM ** -0.5 * math.log2(math.e))
    lane = lax.broadcasted_iota(jnp.int32, q.shape, 1)
    lp = lam_ref[...]
    lam0 = lam0_ref[...]
    lam = (jnp.exp(jnp.sum(lp[0:1] * lp[1:2], axis=1, keepdims=True))
           - jnp.exp(jnp.sum(lp[2:3] * lp[3:4], axis=1, keepdims=True)) + lam0)
    kb = kb_s[...]
    tq = q.shape[0]
    sub = ATTN_Q_SUB if tq % ATTN_Q_SUB == 0 else tq
    chains = [(r0, half) for r0 in range(0, tq, sub) for half in (0, 1)]
    qhs = [jnp.where((lane // A_DIM) == half, q, 0.0).astype(BF16) for half in (0, 1)]
    st = {}

    def scores(c):
        st[c] = _dot_nt(qhs[c[1]][c[0]:c[0] + sub], kb)

    def softmax_pv(c):
        s = st[c]
        e = jnp.exp2(s - jnp.max(s, axis=-1, keepdims=True)).astype(BF16)
        r = _dot(e, va_s[...])
        st[c] = r[:, :A_VDIM] * (1.0 / r[:, A_VDIM:A_VDIM + 1])

    scores(chains[0])
    for j, c in enumerate(chains):
        if j + 1 < len(chains):
            scores(chains[j + 1])
        softmax_pv(c)
    rows = [st[(r0, 0)] - lam * st[(r0, 1)] for r0 in range(0, tq, sub)]
    o = jnp.concatenate(rows, axis=0) if len(rows) > 1 else rows[0]
    y = o * lax.rsqrt(jnp.mean(o * o, axis=-1, keepdims=True) + EPS) * g_ref[...]
    o_ref[0] = (y * (1.0 - lam0)).astype(o_ref.dtype)


def _attention(pa, pa_ctx, lam_p, g_norm, lam0, tables):
    bsz, n, _ = pa.shape
    tq = _tile(n, ATTN_Q_TILE)
    h_off = A_HEADS
    in_specs = [
        pl.BlockSpec((1, tq, LANES), lambda b, h, i: (b, i, h)),
        pl.BlockSpec((1, n, LANES), lambda b, h, i: (b, 0, h_off + h)),
        pl.BlockSpec((1, n, LANES), lambda b, h, i: (b, 0, 2 * h_off + h)),
    ]
    args = [pa, pa, pa]
    with_ctx = pa_ctx is not None
    nc = pa_ctx.shape[1] if with_ctx else 0
    scratch = [pltpu.VMEM((n + nc, LANES), BF16), pltpu.VMEM((n + nc, 2 * A_VDIM), BF16)]
    if with_ctx:
        in_specs += [pl.BlockSpec((1, nc, LANES), lambda b, h, i: (b, 0, h_off + h)),
                     pl.BlockSpec((1, nc, LANES), lambda b, h, i: (b, 0, 2 * h_off + h))]
        args += [pa_ctx, pa_ctx]
    rope = tables is not None
    if rope:
        ct, st = tables
        in_specs += [pl.BlockSpec((tq, LANES), lambda b, h, i: (i, 0))] * 2
        in_specs += [pl.BlockSpec((n, LANES), lambda b, h, i: (0, 0))] * 2
        args += [ct, st, ct, st]
    in_specs += [pl.BlockSpec((4, A_DIM), lambda b, h, i: (0, 0)),
                 pl.BlockSpec((1, 1), lambda b, h, i: (0, 0)),
                 pl.BlockSpec((1, A_VDIM), lambda b, h, i: (0, 0))]
    args += [lam_p, lam0, g_norm]
    return pl.pallas_call(
        functools.partial(_attn_kernel, rope=rope, with_ctx=with_ctx),
        grid=(bsz, A_HEADS, n // tq), in_specs=in_specs,
        out_specs=pl.BlockSpec((1, tq, LANES), lambda b, h, i: (b, i, h)),
        out_shape=jax.ShapeDtypeStruct((bsz, n, A_HEADS * A_VDIM), BF16),
        scratch_shapes=scratch, name="diff_attention",
        compiler_params=_params(("parallel", "parallel", "arbitrary")),
    )(*args)


def _conv_specs(n, tr, tc, col_of):
    r = tr // HALO
    last = n // HALO - 1
    return [
        pl.BlockSpec((1, HALO, tc), lambda b, t, c: (b, jnp.maximum(t * r - 1, 0), col_of(c))),
        pl.BlockSpec((1, tr, tc), lambda b, t, c: (b, t, col_of(c))),
        pl.BlockSpec((1, HALO, tc), lambda b, t, c: (b, jnp.minimum((t + 1) * r, last), col_of(c))),
    ]


def _short_conv(prev_ref, x_ref, next_ref, w, taps):
    t = pl.program_id(1)
    nt = pl.num_programs(1)
    x = x_ref[0]
    tr = x.shape[0]
    prev = jnp.where(t > 0, prev_ref[0], 0.0)
    nxt = jnp.where(t < nt - 1, next_ref[0], 0.0)
    xe = jnp.concatenate([prev, x, nxt], axis=0)
    rows = tr + 2 * HALO
    pad = (taps - 1) // 2
    acc = None
    for j in range(taps):
        d = j - pad
        sh = xe if d == 0 else pltpu.roll(xe, (rows - d) % rows, axis=0)
        term = sh[HALO:HALO + tr] * w[j:j + 1]
        acc = term if acc is None else acc + term
    return acc


def _gdn_prep_kernel(prev_ref, x_ref, next_ref, w_ref, o_ref):
    y = _silu(_short_conv(prev_ref, x_ref, next_ref, w_ref[...], DN_CONV))
    c = pl.program_id(2)
    outs = []
    for h in range(DN_HEADS):
        t = y[:, h * DN_DK:(h + 1) * DN_DK]
        rs = lax.rsqrt(jnp.sum(t * t, axis=-1, keepdims=True) + EPS)
        f = jnp.where(c == 0, rs * (DN_DK ** -0.5), jnp.where(c == 1, rs, 1.0))
        outs.append(t * f)
    o_ref[0] = jnp.concatenate(outs, axis=1)


def _gdn_prep(pd, w_t):
    bsz, n, _ = pd.shape
    tr = _tile(n, 256)
    tc = DN_HEADS * DN_DK
    specs = _conv_specs(n, tr, tc, lambda c: c)
    return pl.pallas_call(
        _gdn_prep_kernel, grid=(bsz, n // tr, 3),
        in_specs=specs + [pl.BlockSpec((8, tc), lambda b, t, c: (0, c))],
        out_specs=pl.BlockSpec((1, tr, tc), lambda b, t, c: (b, t, c)),
        out_shape=jax.ShapeDtypeStruct((bsz, n, 3 * tc), F32),
        name="gdn_prep", compiler_params=_params(("parallel", "parallel", "parallel")),
    )(pd, pd, pd, w_t)


def _gdn_gate_kernel(x_ref, alog_ref, dtb_ref, o_ref, ot_ref):
    x = x_ref[0]
    lane = lax.broadcasted_iota(jnp.int32, x.shape, 1)
    z = x + dtb_ref[...]
    softplus = jnp.maximum(z, 0.0) + jnp.log(1.0 + jnp.exp(-jnp.abs(z)))
    g = -jnp.exp(alog_ref[...]) * softplus
    gb = jnp.where(lane < 2 * DN_HEADS, g, _sigmoid(x))
    o_ref[0] = gb
    ot_ref[0] = gb.T


def _gdn_gates(pab, alog, dtb):
    bsz, n, w = pab.shape
    tr = _tile(n, 512)
    spec = pl.BlockSpec((1, tr, w), lambda b, t: (b, t, 0))
    row = pl.BlockSpec((1, w), lambda b, t: (0, 0))
    return pl.pallas_call(
        _gdn_gate_kernel, grid=(bsz, n // tr), in_specs=[spec, row, row],
        out_specs=[spec, pl.BlockSpec((1, w, tr), lambda b, t: (b, 0, t))],
        out_shape=[jax.ShapeDtypeStruct((bsz, n, w), F32), jax.ShapeDtypeStruct((bsz, w, n), F32)],
        name="gdn_gates", compiler_params=_params(("parallel", "parallel")),
    )(pab, alog, dtb)


GDN_CHUNKS_PER_STEP = 2
GDN_REC_CHUNKS = 4


def _gdn_chunk_kernel(q_ref, k_ref, v_ref, gb_ref, gbt_ref, wq_ref, u_ref, lq_ref, eg_ref):
    cs, nh = DN_CHUNK, DN_HEADS
    r = lax.broadcasted_iota(jnp.int32, (cs, cs), 0)
    c = lax.broadcasted_iota(jnp.int32, (cs, cs), 1)
    eye = jnp.where(r == c, 1.0, 0.0)
    items = []
    for ci in range(GDN_CHUNKS_PER_STEP):
        rows = slice(ci * cs, (ci + 1) * cs)
        gb = gb_ref[0, rows]
        gbt = gbt_ref[0, :, rows]
        q, k, v = q_ref[0, rows], k_ref[0, rows], v_ref[0, rows]
        heads = []
        for h in range(nh):
            sl = slice(h * DN_DK, (h + 1) * DN_DK)
            kh, qh, vh = k[:, sl], q[:, sl], v[:, sl]
            kb = kh.astype(BF16)
            both = _dot_nt(jnp.concatenate([kb, qh.astype(BF16)], axis=0), kb)
            heads.append((kh, qh, vh, both[:cs], both[cs:], kh.T))
        for d in range(2):
            incl = (r >= c) if d == 0 else (r <= c)
            strict = (r > c) if d == 0 else (r < c)
            gcum_c = _dot3(jnp.where(incl, 1.0, 0.0), gb)
            gcum_r = _dot3(gbt, jnp.where(incl, 0.0, 1.0) + eye)
            for h in range(nh):
                kh, qh, vh, kk, qkt, kt = heads[h]
                col = d * nh + h
                gc = gcum_c[:, col:col + 1]
                gr = gcum_r[col:col + 1, :]
                bc = gb[:, 2 * nh + col:2 * nh + col + 1]
                glast = gc[cs - 1:cs, :] if d == 0 else gc[0:1, :]
                decay = jnp.where(incl, jnp.exp(jnp.where(incl, gc - gr, 0.0)), 0.0)
                egc = jnp.exp(gc)
                eg_ref[0, ci, d, h:h + 1, :] = jnp.broadcast_to(jnp.exp(glast), (1, DN_DV))
                wq_ref[0, ci, d, h, cs:2 * cs, :] = (qh * egc).astype(BF16)
                lq_ref[0, ci, d, h, 0:cs, :] = (qkt * decay).astype(BF16)
                lq_ref[0, ci, d, h, cs:, :] = (kt * jnp.exp(glast - gr)).astype(BF16)
                items.append(dict(ci=ci, d=d, h=h, a=jnp.where(strict, bc * kk * decay, 0.0),
                                  rhs=jnp.concatenate([vh * bc, kh * (bc * egc)], axis=1).astype(BF16)))
    masks = []
    for d in range(2):
        lo_r, lo_c = (r, c) if d == 0 else (c, r)
        ms = []
        s = 1
        while s < cs:
            ms.append(((lo_r // (2 * s)) == (lo_c // (2 * s))) & ((lo_r % (2 * s)) >= s) & ((lo_c % (2 * s)) < s))
            s *= 2
        masks.append(ms)
    for it in items:
        it["t"] = eye - jnp.where(masks[it["d"]][0], it["a"], 0.0)
        it["ab"] = it["a"].astype(BF16)
    for lvl in range(1, len(masks[0])):
        for it in items:
            it["x"] = _dot(jnp.where(masks[it["d"]][lvl], it["ab"], jnp.zeros((), BF16)), it["t"].astype(BF16))
        for it in items:
            it["t"] = it["t"] - _dot(it["t"].astype(BF16), it["x"].astype(BF16))
    for it in items:
        ah, al = _split3(it["a"])
        th, tl = _split3(it["t"])
        both = _dot(jnp.concatenate([ah, al], axis=0), th)
        it["res"] = eye - it["t"] - (both[:cs] + both[cs:] + _dot(ah, tl))
    for it in items:
        it["t"] = it["t"] + _dot(it["t"].astype(BF16), it["res"].astype(BF16))
    for it in items:
        uw = _dot(it["t"].astype(BF16), it["rhs"])
        ci, d, h = it["ci"], it["d"], it["h"]
        u_ref[0, ci, d, h] = uw[:, :DN_DV]
        wq_ref[0, ci, d, h, 0:cs, :] = uw[:, DN_DV:].astype(BF16)


def _gdn_chunks(qkv, gb, gbt):
    bsz, n, _ = qkv.shape
    cs = DN_CHUNK
    nc = n // cs
    cps = GDN_CHUNKS_PER_STEP
    w = DN_HEADS * DN_DK
    row = lambda col: pl.BlockSpec((1, cps * cs, w), lambda b, s, col=col: (b, s, col))

    def out(shape):
        return pl.BlockSpec((1, cps) + shape, lambda b, s: (b, s) + (0,) * len(shape))

    lead = (bsz, nc, 2, DN_HEADS)
    return pl.pallas_call(
        _gdn_chunk_kernel, grid=(bsz, nc // cps),
        in_specs=[row(0), row(1), row(2), pl.BlockSpec((1, cps * cs, LANES), lambda b, s: (b, s, 0)),
                  pl.BlockSpec((1, LANES, cps * cs), lambda b, s: (b, 0, s))],
        out_specs=[out((2, DN_HEADS, 2 * cs, DN_DK)), out((2, DN_HEADS, cs, DN_DV)),
                   out((2, DN_HEADS, cs + DN_DK, cs)), out((2, DN_HEADS, DN_DV))],
        out_shape=[jax.ShapeDtypeStruct(lead + (2 * cs, DN_DK), BF16),
                   jax.ShapeDtypeStruct(lead + (cs, DN_DV), F32),
                   jax.ShapeDtypeStruct(lead + (cs + DN_DK, cs), BF16),
                   jax.ShapeDtypeStruct(lead + (DN_DV,), F32)],
        name="gdn_chunks", compiler_params=_params(("parallel", "parallel")),
    )(qkv, qkv, qkv, gb, gbt)


def _gdn_rec_kernel(wqf_ref, uf_ref, lqf_ref, egf_ref, wqb_ref, ub_ref, lqb_ref, egb_ref, s0_ref,
                    of_ref, ob_ref, st_ref):
    cs, nh = DN_CHUNK, DN_HEADS

    @pl.when(pl.program_id(1) == 0)
    def _():
        st_ref[...] = s0_ref[...]

    refs = ((wqf_ref, uf_ref, lqf_ref, egf_ref), (wqb_ref, ub_ref, lqb_ref, egb_ref))
    state = [[st_ref[0, d, h] for h in range(nh)] for d in range(2)]
    for j in range(GDN_REC_CHUNKS):
        chains = []
        for d in range(2):
            ci = j if d == 0 else GDN_REC_CHUNKS - 1 - j
            wq_ref, u_ref, lq_ref, eg_ref = refs[d]
            for h in range(nh):
                chains.append(dict(d=d, h=h, ci=ci, wq=wq_ref[0, ci, 0, h], u=u_ref[0, ci, 0, h],
                                   lq=lq_ref[0, ci, 0, h], eg=eg_ref[0, ci, 0, h:h + 1, :], s=state[d][h]))
        for ch in chains:
            ch["r1"] = _dot(ch["wq"], ch["s"].astype(BF16))
        for ch in chains:
            ch["vn"] = (ch["u"] - ch["r1"][:cs]).astype(BF16)
        for ch in chains:
            ch["r2"] = _dot(ch["lq"], ch["vn"])
        for ch in chains:
            d, h, ci = ch["d"], ch["h"], ch["ci"]
            o_ref = of_ref if d == 0 else ob_ref
            o_ref[0, ci * cs:(ci + 1) * cs, h * DN_DV:(h + 1) * DN_DV] = ch["r1"][cs:] + ch["r2"][:cs]
            state[d][h] = ch["s"] * ch["eg"] + ch["r2"][cs:]
    for d in range(2):
        for h in range(nh):
            st_ref[0, d, h] = state[d][h]


def _gdn_recurrence(wq, u, lq, eg, s0):
    bsz, nc = wq.shape[:2]
    cs = DN_CHUNK * GDN_REC_CHUNKS
    nb = nc // GDN_REC_CHUNKS
    w = DN_HEADS * DN_DK

    def spec(arr, d):
        shape = arr.shape[3:]
        zeros = (0,) * len(shape)
        if d == 0:
            return pl.BlockSpec((1, GDN_REC_CHUNKS, 1) + shape, lambda b, s: (b, s, 0) + zeros)
        return pl.BlockSpec((1, GDN_REC_CHUNKS, 1) + shape, lambda b, s: (b, nb - 1 - s, 1) + zeros)

    st_spec = pl.BlockSpec((1, 2, DN_HEADS, DN_DK, DN_DV), lambda b, s: (b, 0, 0, 0, 0))
    return pl.pallas_call(
        _gdn_rec_kernel, grid=(bsz, nb),
        in_specs=[spec(wq, 0), spec(u, 0), spec(lq, 0), spec(eg, 0),
                  spec(wq, 1), spec(u, 1), spec(lq, 1), spec(eg, 1), st_spec],
        out_specs=[pl.BlockSpec((1, cs, w), lambda b, s: (b, s, 0)),
                   pl.BlockSpec((1, cs, w), lambda b, s: (b, nb - 1 - s, 0)),
                   st_spec],
        out_shape=[jax.ShapeDtypeStruct((bsz, nb * cs, w), F32), jax.ShapeDtypeStruct((bsz, nb * cs, w), F32),
                   jax.ShapeDtypeStruct((bsz, 2, DN_HEADS, DN_DK, DN_DV), F32)],
        name="gdn_recurrence", compiler_params=_params(("parallel", "arbitrary")),
    )(wq, u, lq, eg, wq, u, lq, eg, s0)


def _gdn_post_kernel(of_ref, ob_ref, z_ref, g_ref, y_ref):
    o = of_ref[0] + ob_ref[0]
    z = z_ref[0]
    outs = []
    for h in range(DN_HEADS):
        sl = slice(h * DN_DV, (h + 1) * DN_DV)
        t = o[:, sl]
        outs.append(t * lax.rsqrt(jnp.mean(t * t, axis=-1, keepdims=True) + EPS) * g_ref[...])
    y_ref[0] = (jnp.concatenate(outs, axis=1) * _silu(z)).astype(y_ref.dtype)


def _gdn_post(o_pair, pd, g_norm):
    o_f, o_b = o_pair
    bsz, n, w = o_f.shape
    tr = _tile(n, 256)
    spec = pl.BlockSpec((1, tr, w), lambda b, t: (b, t, 0))
    return pl.pallas_call(
        _gdn_post_kernel, grid=(bsz, n // tr),
        in_specs=[spec, spec, pl.BlockSpec((1, tr, w), lambda b, t: (b, t, 3)),
                  pl.BlockSpec((1, DN_DV), lambda b, t: (0, 0))],
        out_specs=spec, out_shape=jax.ShapeDtypeStruct((bsz, n, w), BF16),
        name="gdn_post", compiler_params=_params(("parallel", "parallel")),
    )(o_f, o_b, pd, g_norm)


def _gdn_segment(pd, pab, conv_t, alog, dtb, s0):
    bsz, n, _ = pd.shape
    qkv = _gdn_prep(pd, conv_t)
    gb, gbt = _gdn_gates(pab, alog, dtb)
    o_f, o_b, s_fin = _gdn_recurrence(*_gdn_chunks(qkv, gb, gbt), s0)
    return (o_f, o_b), s_fin


def _hy_prep_kernel(*refs):
    w_ref, u_ref, x0_ref = refs[9], refs[10], refs[11]
    w = w_ref[...]
    v = _short_conv(refs[0], refs[1], refs[2], w[0], HY_CONV)
    x1 = _short_conv(refs[3], refs[4], refs[5], w[1], HY_CONV)
    x0 = _short_conv(refs[6], refs[7], refs[8], w[2], HY_CONV)
    u_ref[0] = x1 * v
    x0_ref[0] = x0


def _hy_prep(ph, w3):
    bsz, n, _ = ph.shape
    tr = _tile(n, 256)
    tc = 512
    ncb = HY_WIDTH // tc
    specs = []
    for part in range(3):
        specs += _conv_specs(n, tr, tc, lambda c, part=part: part * ncb + c)
    out_spec = pl.BlockSpec((1, tr, tc), lambda b, t, c: (b, t, c))
    return pl.pallas_call(
        _hy_prep_kernel, grid=(bsz, n // tr, ncb), name="hy_prep",
        in_specs=specs + [pl.BlockSpec((3, 8, tc), lambda b, t, c: (0, 0, c))],
        out_specs=[out_spec, out_spec],
        out_shape=[jax.ShapeDtypeStruct((bsz, n, HY_WIDTH), F32)] * 2,
        compiler_params=_params(("parallel", "parallel", "parallel")),
    )(*([ph] * 9), w3)


DFT_BLK = 256


def _dft_tables(n):
    big = 2 * n
    t = jnp.arange(n, dtype=jnp.int32)[None, :]
    angle = lambda k: ((k * t) % big).astype(F32) * (2.0 * math.pi / big)
    k0 = jnp.arange(DFT_BLK, dtype=jnp.int32)[:, None]
    k1 = jnp.arange(0, n, DFT_BLK, dtype=jnp.int32)[:, None]
    c0, s0 = jnp.cos(angle(k0))[None], jnp.sin(angle(k0))[None]
    c1, s1 = jnp.cos(angle(k1))[:, None], jnp.sin(angle(k1))[:, None]
    f_re = c1 * c0 - s1 * s0
    f_im = -(s1 * c0 + c1 * s0)
    nyquist = jnp.broadcast_to((1 - 2 * (t % 2)).astype(F32)[None], f_im.shape)
    f_im = jnp.where((k1[:, None] + k0[None]) == 0, nyquist, f_im)
    return jnp.stack([f_re, f_im], axis=1).reshape(2 * n, n)


def _colabs_kernel(c_ref, o_ref):
    o_ref[...] = jnp.sum(jnp.sum(jnp.abs(c_ref[...]), axis=0), axis=0, keepdims=True)


def _coef_kernel(fh_ref, nrm_ref, o_ref, *, n):
    kidx = pl.program_id(0) * DFT_BLK + lax.broadcasted_iota(jnp.int32, fh_ref.shape[3:], 0)
    sgn = (1 - 2 * (kidx % 2)).astype(F32)
    first = kidx == 0
    fh = fh_ref[:, 0]
    p = fh[0, 0] + sgn * fh[1, 0]
    qraw = fh[0, 1] + jnp.where(first, 1.0, sgn) * fh[1, 1]
    dk = jnp.where(first, 1.0 / (2 * n), 2.0 / (2 * n)) * (1.0 / nrm_ref[...])
    q_m = jnp.where(first, 0.0, qraw)
    o_ref[0] = dk * p
    o_ref[1] = -dk * q_m
    o_ref[2] = dk * q_m
    o_ref[3] = dk * jnp.where(first, qraw, p)


def _hy_filter_coefs(n, f_tab, w1p, b1, w2p, b2, f0, f1, w3, feat, tcol, rates2):
    h = _mm(feat[None], w1p[None], x3=True, extras=[(b1, "row"), (f0, "row")],
            epilogue=lambda acc, b, f: jnp.sin(f * (acc + b)))
    h = _mm(h, w2p[None], x3=True, extras=[(b2, "row"), (f1, "row")],
            epilogue=lambda acc, b, f: jnp.sin(f * (acc + b)))
    h = _mm(h, w3[None], x3=True, extras=[(tcol, "col"), (rates2, "row")],
            epilogue=lambda acc, t, rt: acc * jnp.exp(-t * rt))[0]
    c_lo = h[:, :HY_WIDTH]
    hb = h[:, HY_WIDTH:]
    c_hi = jnp.concatenate([jnp.zeros((1, HY_WIDTH), F32), hb[:0:-1]], axis=0)
    c2 = jnp.stack([c_lo, c_hi])
    tc = 256
    nrm = pl.pallas_call(
        _colabs_kernel, grid=(HY_WIDTH // tc,),
        in_specs=[pl.BlockSpec((2, n, tc), lambda j: (0, 0, j))],
        out_specs=pl.BlockSpec((1, tc), lambda j: (0, j)),
        out_shape=jax.ShapeDtypeStruct((1, HY_WIDTH), F32),
        name="hy_filter_l1", compiler_params=_params(("parallel",)),
    )(c2)
    nb = n // DFT_BLK
    fh = _mm(f_tab[None], c2, x3=True, name="hy_filter_dft").reshape(2, nb, 2, DFT_BLK, HY_WIDTH)
    return pl.pallas_call(
        functools.partial(_coef_kernel, n=n), grid=(nb, HY_WIDTH // tc),
        in_specs=[pl.BlockSpec((2, 1, 2, DFT_BLK, tc), lambda t, j: (0, t, 0, 0, j)),
                  pl.BlockSpec((1, tc), lambda t, j: (0, j))],
        out_specs=pl.BlockSpec((4, DFT_BLK, tc), lambda t, j: (0, t, j)),
        out_shape=jax.ShapeDtypeStruct((4, n, HY_WIDTH), F32),
        name="hy_filter_coef", compiler_params=_params(("parallel", "parallel")),
    )(fh, nrm)


def _freq_mul(acc, cf):
    a, b = acc[:DFT_BLK], acc[DFT_BLK:]
    return jnp.concatenate([a * cf[0] + b * cf[1], a * cf[2] + b * cf[3]], axis=0)


def _hyena(ph, conv3, coef, f_tab, ft_tab, d_skip):
    bsz, n, _ = ph.shape
    u, x0 = _hy_prep(ph, conv3)
    coef_spec = lambda tm, tn: pl.BlockSpec((1, 4, DFT_BLK, tn), lambda gi, j, i, kk: (0, 0, i, j))
    y = _mm(f_tab[None], u, out_dtype=BF16, tm=2 * DFT_BLK, tn=HY_WIDTH, extras=[(coef[None], coef_spec)],
            epilogue=_freq_mul, name="hy_dft_filter")
    return _mm(ft_tab[None], y, out_dtype=BF16, tn=HY_WIDTH,
               extras=[(u, "tile"), (x0, "tile"), (d_skip, "row")],
               epilogue=lambda acc, uu, xx, dd: xx * (acc + dd * uu), name="hy_idft_gate")


def _merge_kernel(ya_ref, yd_ref, yh_ref, g0_ref, g1_ref, g2_ref, w_ref, o_ref):
    acc = _sigmoid(g0_ref[...]) * _dot(ya_ref[...], w_ref[0])
    acc = acc + _sigmoid(g1_ref[...]) * _dot(yd_ref[...], w_ref[1])
    acc = acc + _sigmoid(g2_ref[...]) * _dot(yh_ref[...], w_ref[2])
    o_ref[...] = acc.astype(o_ref.dtype)


def _merge(ya, yd, yh, pg, w_branch):
    m = ya.shape[0]
    tm = _tile(m, 512)
    tn = 512
    nj = D_MODEL // tn
    br = pl.BlockSpec((tm, ya.shape[1]), lambda j, i: (i, 0))
    gate = lambda r: pl.BlockSpec((tm, tn), lambda j, i, r=r: (i, r * nj + j))
    return pl.pallas_call(
        _merge_kernel, grid=(nj, m // tm), name="branch_merge",
        in_specs=[br, br, br, gate(0), gate(1), gate(2),
                  pl.BlockSpec((N_BRANCH, ya.shape[1], tn), lambda j, i: (0, 0, j))],
        out_specs=pl.BlockSpec((tm, tn), lambda j, i: (i, j)),
        out_shape=jax.ShapeDtypeStruct((m, D_MODEL), BF16),
        compiler_params=_params(("parallel", "parallel")),
    )(ya, yd, yh, pg, pg, pg, w_branch)


def _select_kernel(aff_ref, tri_ref, slot_ref, *, cap):
    aff = aff_ref[0]
    bits = pltpu.bitcast(aff, jnp.int32)
    thr = jnp.zeros((aff.shape[0], 1), jnp.int32)
    for bit in range(30, -1, -1):
        cand = thr | (1 << bit)
        cnt = jnp.sum(jnp.where(bits >= cand, 1.0, 0.0), axis=1, keepdims=True)
        thr = jnp.where(cnt >= cap, cand, thr)
    gt = bits > thr
    eq = bits == thr
    need = cap - jnp.sum(jnp.where(gt, 1.0, 0.0), axis=1, keepdims=True)
    tri = tri_ref[...]
    eq_before = _dot(jnp.where(eq, 1.0, 0.0).astype(BF16), tri)
    sel = gt | (eq & (eq_before < need))
    slot = _dot(jnp.where(sel, 1.0, 0.0).astype(BF16), tri)
    slot_ref[0] = jnp.where(sel, slot, -1.0).astype(jnp.int32)


def _select(aff_t, tri, cap):
    bsz, ne, n = aff_t.shape
    return pl.pallas_call(
        functools.partial(_select_kernel, cap=cap), grid=(bsz,), name="ffn_select",
        in_specs=[pl.BlockSpec((1, ne, n), lambda b: (b, 0, 0)), pl.BlockSpec((n, n), lambda b: (0, 0))],
        out_specs=pl.BlockSpec((1, ne, n), lambda b: (b, 0, 0)),
        out_shape=jax.ShapeDtypeStruct((bsz, ne, n), jnp.int32),
        compiler_params=_params(("parallel",)),
    )(aff_t, tri)


def _gather_kernel(slot_ref, aff_ref, h_ref, xe_ref, gs_ref, *, cap):
    e = pl.program_id(1)
    slot = slot_ref[0, pl.ds(e, 1), :]
    aff = aff_ref[0, pl.ds(e, 1), :]
    n = slot.shape[1]
    hit = lax.broadcasted_iota(jnp.int32, (cap, n), 0) == slot
    xe_ref[0, 0] = _dot(jnp.where(hit, 1.0, 0.0).astype(BF16), h_ref[0]).astype(xe_ref.dtype)
    gs_ref[0, 0] = jnp.sum(jnp.where(hit, aff, 0.0), axis=1, keepdims=True)


def _gather(slot, aff_t, h, cap):
    bsz, ne, n = slot.shape
    d = h.shape[2]
    row = pl.BlockSpec((1, ne, n), lambda b, e: (b, 0, 0))
    return pl.pallas_call(
        functools.partial(_gather_kernel, cap=cap), grid=(bsz, ne), name="ffn_gather",
        in_specs=[row, row, pl.BlockSpec((1, n, d), lambda b, e: (b, 0, 0))],
        out_specs=[pl.BlockSpec((1, 1, cap, d), lambda b, e: (e, b, 0, 0)),
                   pl.BlockSpec((1, 1, cap, 1), lambda b, e: (e, b, 0, 0))],
        out_shape=[jax.ShapeDtypeStruct((ne, bsz, cap, d), BF16),
                   jax.ShapeDtypeStruct((ne, bsz, cap, 1), F32)],
        compiler_params=_params(("parallel", "arbitrary")),
    )(slot, aff_t, h)


def _expert_kernel(x_ref, gs_ref, wg_ref, wu_ref, wd_ref, y_ref):
    x = x_ref[0]
    a = _dot(x, wg_ref[0])
    u = _dot(x, wu_ref[0])
    y = _dot((_silu(a) * u).astype(BF16), wd_ref[0])
    y_ref[0] = (y * gs_ref[0]).astype(y_ref.dtype)


def _experts(xe, gs, wg, wu, wd):
    ne, m, d = xe.shape
    ff = wg.shape[2]
    tm = _tile(m, 512)
    return pl.pallas_call(
        _expert_kernel, grid=(ne, m // tm), name="ffn_experts",
        in_specs=[pl.BlockSpec((1, tm, d), lambda e, i: (e, i, 0)),
                  pl.BlockSpec((1, tm, 1), lambda e, i: (e, i, 0)),
                  pl.BlockSpec((1, d, ff), lambda e, i: (e, 0, 0)),
                  pl.BlockSpec((1, d, ff), lambda e, i: (e, 0, 0)),
                  pl.BlockSpec((1, ff, d), lambda e, i: (e, 0, 0))],
        out_specs=pl.BlockSpec((1, tm, d), lambda e, i: (e, i, 0)),
        out_shape=jax.ShapeDtypeStruct((ne, m, d), BF16),
        compiler_params=_params(("parallel", "parallel")),
    )(xe, gs, wg, wu, wd)


def _scatter_kernel(slot_ref, ye_ref, x_ref, g_ref, o_ref, *, cap):
    ne = slot_ref.shape[1]
    tr = slot_ref.shape[2]
    lane = lax.broadcasted_iota(jnp.int32, (tr, cap), 1).astype(F32)
    slot_t = slot_ref[0].astype(F32).T
    hit = jnp.concatenate([jnp.where(lane == slot_t[:, e:e + 1], 1.0, 0.0).astype(BF16) for e in range(ne)],
                          axis=1)
    ye = ye_ref[...].reshape(ne * cap, ye_ref.shape[3])
    o_ref[0] = x_ref[0] + g_ref[0] * _dot(hit, ye)


def _scatter_residual(slot, ye, x, gate_row, cap):
    bsz, ne, n = slot.shape
    d = x.shape[2]
    tr = _tile(n, 256)
    xs = pl.BlockSpec((1, tr, d), lambda b, i: (b, i, 0))
    return pl.pallas_call(
        functools.partial(_scatter_kernel, cap=cap), grid=(bsz, n // tr),
        in_specs=[pl.BlockSpec((1, ne, tr), lambda b, i: (b, 0, i)),
                  pl.BlockSpec((ne, 1, cap, d), lambda b, i: (0, b, 0, 0)),
                  xs, pl.BlockSpec((1, 1, d), lambda b, i: (b, 0, 0))],
        out_specs=xs, out_shape=jax.ShapeDtypeStruct((bsz, n, d), F32),
        name="ffn_scatter", compiler_params=_params(("parallel", "arbitrary")),
    )(slot, ye, x, gate_row)


def _ffn(x, gain, mod, router_t, wg, wu, wd, tri):
    bsz, n, d = x.shape
    cap = EC_CAPACITY * n // N_EXPERTS
    h, aff_t = _normmod(x, gain, mod, 3, 4, router_t)
    slot = _select(aff_t, tri, cap)
    xe, gs = _gather(slot, aff_t, h, cap)
    ye = _experts(xe.reshape(N_EXPERTS, bsz * cap, d), gs.reshape(N_EXPERTS, bsz * cap, 1), wg, wu, wd)
    return _scatter_residual(slot, ye.reshape(N_EXPERTS, bsz, cap, d), x, mod[:, 5:6], cap)


def _hy_consts(n):
    t = jnp.linspace(0.0, 1.0, n, dtype=F32)[:, None]
    pos = jnp.arange(n, dtype=F32)
    bands = jnp.linspace(1e-4, HY_BANDS - 1, HY_BANDS, dtype=F32)
    ang = (2.0 * math.pi / n) * pos[:, None] * bands[None, :]
    feat = jnp.concatenate([t, jnp.cos(ang), -jnp.sin(ang)], axis=-1)
    feat = jnp.pad(feat, ((0, 0), (0, LANES - HY_EMB)))
    f_tab = _dft_tables(n)
    return dict(feat=feat, tcol=t[None], f_tab=f_tab, f_bf=f_tab.astype(BF16),
                ft_bf=jnp.transpose(f_tab).astype(BF16))


PROJ_GROUPS = ((0, 3072), (3072, 4096), (7168, 32), (7200, 3072), (10272, 6144))


def _in_proj_weights_kernel(w_ref, *outs):
    for (off, width), o_ref in zip(PROJ_GROUPS, outs):
        blk = w_ref[0, :, off:off + width].astype(BF16)
        if width < o_ref.shape[1]:
            blk = jnp.concatenate([blk, jnp.zeros((blk.shape[0], o_ref.shape[1] - width), BF16)], axis=1)
        o_ref[...] = blk


def _in_proj_weights(w_in, layer):
    _, d, total = w_in.shape
    tr = 256
    widths = [max(w, LANES) for _, w in PROJ_GROUPS]
    return pl.pallas_call(
        _in_proj_weights_kernel, grid=(d // tr,),
        in_specs=[pl.BlockSpec((1, tr, total), lambda i: (layer, i, 0))],
        out_specs=[pl.BlockSpec((tr, w), lambda i: (i, 0)) for w in widths],
        out_shape=[jax.ShapeDtypeStruct((d, w), BF16) for w in widths],
        name="in_proj_weights", compiler_params=_params(("parallel",)),
    )(w_in)


def _mixer_segment_proj(h, weights):
    bsz, n, d = h.shape
    hf = h.reshape(1, bsz * n, d)
    return [None if w is None else
            _mm(hf, w[None], tm=1024, tn=1024, name="in_proj").reshape(bsz, n, w.shape[1]) for w in weights]


def kernel(x, c, ctx, c_ctx, w_ada, b_ada, norm_mix, norm_ffn, w_in, diff_lambda, diff_norm, dn_conv,
           dn_a_log, dn_dt_bias, dn_norm, hy_conv, hy_w1, hy_b1, hy_w2, hy_b2, hy_freq, hy_w3, hy_bias,
           w_branch, w_out, router, w_gate, w_up, w_down, norm_final):
    depth = w_ada.shape[0]
    bsz, n_lat, d = x.shape
    n_ctx = ctx.shape[1]
    assert d == D_MODEL and n_lat % DFT_BLK == 0 and n_ctx % DFT_BLK == 0 and DFT_BLK % DN_CHUNK == 0

    rope_tabs = _rope_tables(n_lat)
    hy_l = _hy_consts(n_lat)
    hy_c = _hy_consts(n_ctx)
    tri_l = (jnp.arange(n_lat)[:, None] < jnp.arange(n_lat)[None, :]).astype(BF16)
    tri_c = (jnp.arange(n_ctx)[:, None] < jnp.arange(n_ctx)[None, :]).astype(BF16)
    rates = jnp.abs(jnp.linspace(HY_DECAY_MIN, HY_DECAY_MAX, HY_WIDTH, dtype=F32))
    rates2 = jnp.concatenate([rates, rates])[None, None]
    cc = jnp.concatenate([c, jnp.broadcast_to(c_ctx[None], (8, d))], axis=0)[None]

    for l in range(depth):
        with_ctx = l < depth - 1
        lam0 = jnp.full((1, 1), 0.8 - 0.6 * math.exp(-0.3 * l), F32)
        mod = _mm(cc, w_ada, b_sel=l, tn=1024, prologue=_silu, extras=[(b_ada[l][None, None], "row")],
                  epilogue=lambda acc, bias: acc + bias, name="ada_ln")[0]
        mod_l = jnp.pad(mod[:bsz].reshape(bsz, 6, d), ((0, 0), (0, 2), (0, 0)))
        mod_c = jnp.broadcast_to(jnp.pad(mod[bsz].reshape(1, 6, d), ((0, 0), (0, 2), (0, 0))), (bsz, 8, d))

        w_attn, w_dn, w_ab, w_hy, w_gt = _in_proj_weights(w_in, l)
        wb = w_branch[l].astype(BF16)
        wo = w_out[l].astype(BF16)
        wg, wu, wd = w_gate[l].astype(BF16), w_up[l].astype(BF16), w_down[l].astype(BF16)
        router_t = jnp.transpose(router[l])
        gain_mix, gain_ffn = norm_mix[l][None], norm_ffn[l][None]
        conv_dn = jnp.pad(jnp.transpose(dn_conv[l]), ((0, 8 - DN_CONV), (0, 0)))
        conv_hy = jnp.pad(jnp.transpose(hy_conv[l]), ((0, 8 - HY_CONV), (0, 0)))
        conv_hy = jnp.transpose(conv_hy.reshape(8, 3, HY_WIDTH), (1, 0, 2))
        alog = jnp.pad(dn_a_log[l].reshape(1, -1), ((0, 0), (0, LANES - 2 * DN_HEADS)))
        dtb = jnp.pad(dn_dt_bias[l].reshape(1, -1), ((0, 0), (0, LANES - 2 * DN_HEADS)))
        pad_h = lambda w: jnp.pad(w, ((0, LANES - w.shape[0]), (0, LANES - w.shape[1])))
        pad_r = lambda v: jnp.pad(v, (0, LANES - v.shape[0]))[None, None]
        w1p = pad_h(hy_w1[l])
        w2p = pad_h(hy_w2[l])
        w3p = jnp.pad(hy_w3[l], ((0, LANES - HY_HIDDEN), (0, 0)))
        filt = lambda n, hc: _hy_filter_coefs(
            n, hc["f_tab"], w1p, pad_r(hy_b1[l]), w2p, pad_r(hy_b2[l]), pad_r(hy_freq[l, 0]),
            pad_r(hy_freq[l, 1]), w3p, hc["feat"], hc["tcol"], rates2)
        d_skip = hy_bias[l][None, None]
        g_dn = dn_norm[l][None]
        g_diff = diff_norm[l][None]

        h_l = _normmod(x, gain_mix, mod_l, 0, 1)
        h_c = _normmod(ctx, gain_mix, mod_c, 0, 1)
        pa_l, pd_l, ph_l, pg_l, pab_l = _mixer_segment_proj(h_l, (w_attn, w_dn, w_hy, w_gt, w_ab))
        pa_c, pd_c, ph_c, pg_c, pab_c = _mixer_segment_proj(
            h_c, (w_attn, w_dn, w_hy if with_ctx else None, w_gt if with_ctx else None, w_ab))

        ya_l = _attention(pa_l, pa_c, diff_lambda[l], g_diff, lam0, rope_tabs)
        s0 = jnp.zeros((bsz, 2, DN_HEADS, DN_DK, DN_DV), F32)
        o_c, s_c = _gdn_segment(pd_c, pab_c, conv_dn, alog, dtb, s0)
        o_l, _ = _gdn_segment(pd_l, pab_l, conv_dn, alog, dtb, s_c)
        yd_l = _gdn_post(o_l, pd_l, g_dn)
        yh_l = _hyena(ph_l, conv_hy, filt(n_lat, hy_l), hy_l["f_bf"], hy_l["ft_bf"], d_skip)

        def merge_out(ya, yd, yh, pg, xs, mod_s):
            n = xs.shape[1]
            flat = lambda t: t.reshape(bsz * n, t.shape[2])
            y = _merge(flat(ya), flat(yd), flat(yh), flat(pg), wb).reshape(bsz, n, d)
            return _mm(y, wo[None], tm=1024, tn=1024, extras=[(xs, "tile"), (mod_s[:, 2:3], "row")],
                       epilogue=lambda acc, xx, gg: xx + gg * acc, name="out_proj_residual")

        x = merge_out(ya_l, yd_l, yh_l, pg_l, x, mod_l)
        x = _ffn(x, gain_ffn, mod_l, router_t, wg, wu, wd, tri_l)
        if with_ctx:
            ya_c = _attention(pa_c, None, diff_lambda[l], g_diff, lam0, None)
            yd_c = _gdn_post(o_c, pd_c, g_dn)
            yh_c = _hyena(ph_c, conv_hy, filt(n_ctx, hy_c), hy_c["f_bf"], hy_c["ft_bf"], d_skip)
            ctx = merge_out(ya_c, yd_c, yh_c, pg_c, ctx, mod_c)
            ctx = _ffn(ctx, gain_ffn, mod_c, router_t, wg, wu, wd, tri_c)
    return _final_norm(x, norm_final[None])
```
